```python
import jax, jax.numpy as jnp
from jax import lax
import numpy as np

D_MODEL = 1024
BATCH = 8
SEQ = 4096
DEPTH = 1

CHUNK = 64
Q_BLOCK = 128

MIX_WIDTH = D_MODEL
GLA_WIDTH = MIX_WIDTH // 2
DSA_WIDTH = MIX_WIDTH - GLA_WIDTH

GLA_HEADS = 4
GLA_DV = GLA_WIDTH // GLA_HEADS
GLA_DK = GLA_DV // 2
GLA_QK = GLA_HEADS * GLA_DK
GLA_LOWRANK = 16
GLA_TAU = 16.0

DSA_HEADS = 8
DSA_DH = DSA_WIDTH // DSA_HEADS
IDX_HEADS = 8
IDX_DIM = 64
TOPK_MAX = 256

D_FF = 2816

LN_EPS = 1e-5
RMS_EPS = 1e-6
DEEPNORM_ALPHA = (2.0 * DEPTH) ** 0.25
DEEPNORM_BETA = (8.0 * DEPTH) ** -0.25

IN_SPLITS = (GLA_QK, GLA_QK, GLA_WIDTH, GLA_WIDTH, GLA_LOWRANK,
             DSA_WIDTH, DSA_WIDTH, DSA_WIDTH, IDX_HEADS * IDX_DIM, IDX_DIM, IDX_HEADS)
IN_TOTAL = 2 * GLA_QK + 2 * GLA_WIDTH + GLA_LOWRANK + 3 * DSA_WIDTH + IDX_HEADS * IDX_DIM + IDX_DIM + IDX_HEADS

kernel_name = "hymba_gla_dsa_macaron_deepnorm"


def layer_norm(x, g, b):
    xf = x.astype(jnp.float32)
    mu = jnp.mean(xf, axis=-1, keepdims=True)
    xc = xf - mu
    var = jnp.mean(xc * xc, axis=-1, keepdims=True)
    return (xc * lax.rsqrt(var + LN_EPS) * g + b).astype(x.dtype)


def swiglu_ffn(x, w_gu, w_down):
    gu = x @ w_gu
    g, u = jnp.split(gu, 2, axis=-1)
    return (jax.nn.silu(g) * u) @ w_down


def gla_chunked(q, k, v, log_a):
    B, T, H, dk = q.shape
    dv = v.shape[-1]
    n = T // CHUNK

    def to_chunks(t):
        return t.reshape(B, n, CHUNK, H, t.shape[-1]).transpose(1, 0, 3, 2, 4)

    qc, kc, vc, ac = to_chunks(q), to_chunks(k), to_chunks(v), to_chunks(log_a)
    causal = jnp.tril(jnp.ones((CHUNK, CHUNK), dtype=bool))

    def step(S, inp):
        qn, kn, vn, an = inp
        b = jnp.cumsum(an, axis=-2)
        b_last = b[:, :, -1:, :]
        diff = b[:, :, :, None, :] - b[:, :, None, :, :]
        diff = jnp.where(causal[None, None, :, :, None], diff, -jnp.inf)
        A = jnp.einsum('bhid,bhjd,bhijd->bhij', qn, kn, jnp.exp(diff))
        o = jnp.einsum('bhij,bhjv->bhiv', A, vn) + jnp.einsum('bhid,bhdv->bhiv', qn * jnp.exp(b), S)
        S_new = jnp.exp(b_last[:, :, 0, :])[..., None] * S + jnp.einsum(
            'bhjd,bhjv->bhdv', kn * jnp.exp(b_last - b), vn)
        return S_new, o

    S0 = jnp.zeros((B, H, dk, dv), jnp.float32)
    _, o = lax.scan(step, S0, (qc, kc, vc, ac))
    return o.transpose(1, 0, 3, 2, 4).reshape(B, T, H, dv)


def dsa_attention(q, k, v, q_idx, k_idx, w_idx):
    B, T, H, Dh = q.shape
    topk = min(TOPK_MAX, T // 4)
    nb = T // Q_BLOCK
    key_chunk = jnp.arange(T) // CHUNK
    k_idx32 = k_idx.astype(jnp.float32)
    scale = Dh ** -0.5

    def to_blocks(t):
        return jnp.moveaxis(t.reshape((B, nb, Q_BLOCK) + t.shape[2:]), 1, 0)

    def gather_rows(table, idx):
        return jax.vmap(lambda tb, ib: tb[ib])(table, idx)

    def block(args):
        qb, qib, wb, start = args
        q_chunk = (start + jnp.arange(Q_BLOCK)) // CHUNK
        adm = key_chunk[None, :] <= q_chunk[:, None]
        s = jax.nn.relu(jnp.einsum('bqhd,bsd->bqhs', qib.astype(jnp.float32), k_idx32))
        score = jnp.einsum('bqh,bqhs->bqs', wb.astype(jnp.float32), s)
        score = jnp.where(adm[None], score, -jnp.inf)
        _, idx = lax.top_k(score, topk)
        valid = key_chunk[idx] <= q_chunk[None, :, None]
        kg = gather_rows(k, idx)
        vg = gather_rows(v, idx)
        logits = jnp.einsum('bqhd,bqkhd->bqhk', qb, kg).astype(jnp.float32) * scale
        logits = jnp.where(valid[:, :, None, :], logits, -jnp.inf)
        p = jax.nn.softmax(logits, axis=-1)
        return jnp.einsum('bqhk,bqkhd->bqhd', p.astype(vg.dtype), vg)

    starts = jnp.arange(nb) * Q_BLOCK
    out = lax.map(block, (to_blocks(q), to_blocks(q_idx), to_blocks(w_idx), starts))
    return jnp.moveaxis(out, 0, 1).reshape(B, T, H * Dh)


def hybrid_mixer(h, w_in, w_gla_a2, b_gla_a, gla_norm_g, w_out):
    B, T, _ = h.shape
    f32 = jnp.float32
    proj = h @ w_in
    offs = [int(o) for o in np.cumsum(IN_SPLITS)[:-1]]
    gq, gk, gv, gg, ga, dq, dk, dv, iq, ik, iw = jnp.split(proj, offs, axis=-1)

    q = gq.astype(f32).reshape(B, T, GLA_HEADS, GLA_DK) * (GLA_DK ** -0.5)
    k = gk.astype(f32).reshape(B, T, GLA_HEADS, GLA_DK)
    v = gv.astype(f32).reshape(B, T, GLA_HEADS, GLA_DV)
    log_a = (jax.nn.log_sigmoid((ga @ w_gla_a2 + b_gla_a).astype(f32)) / GLA_TAU).reshape(B, T, GLA_HEADS, GLA_DK)
    o = gla_chunked(q, k, v, log_a)
    o = o * lax.rsqrt(jnp.mean(o * o, axis=-1, keepdims=True) + RMS_EPS) * gla_norm_g.astype(f32)
    o_gla = (o.reshape(B, T, GLA_WIDTH) * jax.nn.silu(gg.astype(f32))).astype(h.dtype)

    o_dsa = dsa_attention(
        dq.reshape(B, T, DSA_HEADS, DSA_DH),
        dk.reshape(B, T, DSA_HEADS, DSA_DH),
        dv.reshape(B, T, DSA_HEADS, DSA_DH),
        iq.reshape(B, T, IDX_HEADS, IDX_DIM) * (IDX_DIM ** -0.5),
        ik,
        iw * (IDX_HEADS ** -0.5),
    ).astype(h.dtype)

    return jnp.concatenate([o_gla, o_dsa], axis=-1) @ w_out


def setup_inputs(seed: int = 0) -> dict:
    key = jax.random.key(seed)
    ks = jax.random.split(key, 17)
    nrm = lambda k, shape, fan_in, gain=1.0: jax.random.normal(k, shape, jnp.float32) * (gain * fan_in ** -0.5)
    return {
        "x": jax.random.normal(ks[0], (BATCH, SEQ, D_MODEL), jnp.float32),
        "w_in": nrm(ks[1], (DEPTH, D_MODEL, IN_TOTAL), D_MODEL),
        "w_gla_a2": nrm(ks[2], (DEPTH, GLA_LOWRANK, GLA_QK), GLA_LOWRANK),
        "b_gla_a": 0.1 * jax.random.normal(ks[3], (DEPTH, GLA_QK), jnp.float32),
        "gla_norm_g": 1.0 + 0.02 * jax.random.normal(ks[4], (DEPTH, GLA_DV), jnp.float32),
        "w_out": nrm(ks[5], (DEPTH, MIX_WIDTH, D_MODEL), MIX_WIDTH, DEEPNORM_BETA),
        "ffn1_w_gu": nrm(ks[6], (DEPTH, D_MODEL, 2 * D_FF), D_MODEL),
        "ffn1_w_down": nrm(ks[7], (DEPTH, D_FF, D_MODEL), D_FF, DEEPNORM_BETA),
        "ffn2_w_gu": nrm(ks[8], (DEPTH, D_MODEL, 2 * D_FF), D_MODEL),
        "ffn2_w_down": nrm(ks[9], (DEPTH, D_FF, D_MODEL), D_FF, DEEPNORM_BETA),
        "ln1_g": 1.0 + 0.02 * jax.random.normal(ks[10], (DEPTH, D_MODEL), jnp.float32),
        "ln1_b": 0.02 * jax.random.normal(ks[11], (DEPTH, D_MODEL), jnp.float32),
        "ln2_g": 1.0 + 0.02 * jax.random.normal(ks[12], (DEPTH, D_MODEL), jnp.float32),
        "ln2_b": 0.02 * jax.random.normal(ks[13], (DEPTH, D_MODEL), jnp.float32),
        "ln3_g": 1.0 + 0.02 * jax.random.normal(ks[14], (DEPTH, D_MODEL), jnp.float32),
        "ln3_b": 0.02 * jax.random.normal(ks[15], (DEPTH, D_MODEL), jnp.float32),
    }


def reference(x, w_in, w_gla_a2, b_gla_a, gla_norm_g, w_out, ffn1_w_gu, ffn1_w_down,
              ffn2_w_gu, ffn2_w_down, ln1_g, ln1_b, ln2_g, ln2_b, ln3_g, ln3_b):
    for l in range(DEPTH):
        x = layer_norm(DEEPNORM_ALPHA * x + 0.5 * swiglu_ffn(x, ffn1_w_gu[l], ffn1_w_down[l]), ln1_g[l], ln1_b[l])
        x = layer_norm(DEEPNORM_ALPHA * x + hybrid_mixer(x, w_in[l], w_gla_a2[l], b_gla_a[l], gla_norm_g[l], w_out[l]),
                       ln2_g[l], ln2_b[l])
        x = layer_norm(DEEPNORM_ALPHA * x + 0.5 * swiglu_ffn(x, ffn2_w_gu[l], ffn2_w_down[l]), ln3_g[l], ln3_b[l])
    return x
```

```python
import functools

import jax
import jax.numpy as jnp
from jax import lax
from jax.experimental import pallas as pl
from jax.experimental.pallas import tpu as pltpu

F32 = jnp.float32
BF16 = jnp.bfloat16

D_MODEL = 1024
D_FF = 2816
CHUNK = 64
GLA_HEADS = 4
GLA_DK = 64
GLA_DV = 128
GLA_QK = GLA_HEADS * GLA_DK
GLA_WIDTH = GLA_HEADS * GLA_DV
GLA_LOWRANK = 16
GLA_TAU = 16.0
DSA_HEADS = 8
DSA_DH = 64
DSA_WIDTH = DSA_HEADS * DSA_DH
IDX_HEADS = 8
IDX_DIM = 64
TOPK_MAX = 256
LN_EPS = 1e-5
RMS_EPS = 1e-6
DEPTH = 1
DEEPNORM_ALPHA = (2.0 * DEPTH) ** 0.25

LANES = 128
VMEM_LIMIT_BYTES = 56 * 1024 * 1024

FFN_TM = 512
FFN_FC = 256
PROJ_TM = 512
GLA_TT = 512
GLA_C = 128
DSA_TQ = 128
DSA_TK = PROJ_TM
NEG_BIG = -1e30
INT_MIN = -(2 ** 31)

_NT = (((1,), (1,)), ((), ()))


def _layer_norm(y, g, b):
    mu = jnp.mean(y, axis=-1, keepdims=True)
    yc = y - mu
    var = jnp.mean(yc * yc, axis=-1, keepdims=True)
    return yc * lax.rsqrt(var + LN_EPS) * g + b


def _silu(x):
    return x * jax.nn.sigmoid(x)


def _ffn_ln_kernel(x_ref, wg_ref, wu_ref, wd_ref, g_ref, b_ref, o_ref, xb_ref, acc_ref):
    x = x_ref[...]
    xb_ref[...] = x.astype(BF16)
    acc_ref[...] = jnp.zeros_like(acc_ref)

    def body(c, carry):
        xb = xb_ref[...]
        g = jnp.dot(xb, wg_ref[c], preferred_element_type=F32)
        u = jnp.dot(xb, wu_ref[c], preferred_element_type=F32)
        a = (_silu(g) * u).astype(BF16)
        acc_ref[...] += jnp.dot(a, wd_ref[c], preferred_element_type=F32)
        return carry

    lax.fori_loop(0, D_FF // FFN_FC, body, 0)
    y = DEEPNORM_ALPHA * x + 0.5 * acc_ref[...]
    o_ref[...] = _layer_norm(y, g_ref[...], b_ref[...])


def _ffn_ln(x2d, w_gu, w_down, g, b):
    n = x2d.shape[0]
    nc = D_FF // FFN_FC
    wg = w_gu[:, :D_FF].astype(BF16).reshape(D_MODEL, nc, FFN_FC).transpose(1, 0, 2)
    wu = w_gu[:, D_FF:].astype(BF16).reshape(D_MODEL, nc, FFN_FC).transpose(1, 0, 2)
    wd = w_down.astype(BF16).reshape(nc, FFN_FC, D_MODEL)
    full3 = lambda i: (0, 0, 0)
    return pl.pallas_call(
        _ffn_ln_kernel,
        name="ffn_ln",
        grid=(n // FFN_TM,),
        in_specs=[
            pl.BlockSpec((FFN_TM, D_MODEL), lambda i: (i, 0)),
            pl.BlockSpec((nc, D_MODEL, FFN_FC), full3, pipeline_mode=pl.Buffered(1)),
            pl.BlockSpec((nc, D_MODEL, FFN_FC), full3, pipeline_mode=pl.Buffered(1)),
            pl.BlockSpec((nc, FFN_FC, D_MODEL), full3, pipeline_mode=pl.Buffered(1)),
            pl.BlockSpec((1, D_MODEL), lambda i: (0, 0)),
            pl.BlockSpec((1, D_MODEL), lambda i: (0, 0)),
        ],
        out_specs=pl.BlockSpec((FFN_TM, D_MODEL), lambda i: (i, 0)),
        out_shape=jax.ShapeDtypeStruct((n, D_MODEL), F32),
        scratch_shapes=[
            pltpu.VMEM((FFN_TM, D_MODEL), BF16),
            pltpu.VMEM((FFN_TM, D_MODEL), F32),
        ],
        compiler_params=pltpu.CompilerParams(
            dimension_semantics=("arbitrary",), vmem_limit_bytes=VMEM_LIMIT_BYTES),
    )(x2d, wg, wu, wd, g.reshape(1, D_MODEL), b.reshape(1, D_MODEL))


_N_GQ, _N_GK, _N_GV, _N_GG, _N_GA, _N_DQ, _N_DV, _N_IQ, _N_IW, _N_END = (
    0, 256, 512, 1024, 1536, 1664, 2176, 2688, 3200, 3328)
_T_GV, _T_DK, _T_IK, _T_END = 0, 512, 1024, 1152


def _proj_kernel(x_ref, wn_ref, wt_ref, wa2_ref, ba_ref,
                 gq_ref, gk_ref, la_ref, gv_ref, gg_ref, dq_ref, dv_ref, iq_ref, iw_ref,
                 gvT_ref, dkT_ref, ikT_ref):
    xb = x_ref[0].astype(BF16)

    def nproj(lo, hi):
        return jnp.dot(xb, wn_ref[:, lo:hi], preferred_element_type=F32)

    def tproj(lo, hi):
        return lax.dot_general(wt_ref[lo:hi, :], xb, _NT, preferred_element_type=F32)

    gq_ref[0] = nproj(_N_GQ, _N_GK)
    gk_ref[0] = nproj(_N_GK, _N_GV)
    gv_ref[0] = nproj(_N_GV, _N_GG).astype(BF16)
    gg_ref[0] = nproj(_N_GG, _N_GA)
    ga = nproj(_N_GA, _N_DQ)
    z = jnp.dot(ga.astype(BF16), wa2_ref[...], preferred_element_type=F32) + ba_ref[...]
    la_ref[0] = jax.nn.log_sigmoid(z) * (1.0 / GLA_TAU)
    dq_ref[0] = (nproj(_N_DQ, _N_DV) * (DSA_DH ** -0.5)).astype(BF16)
    dv_ref[0] = nproj(_N_DV, _N_IQ).astype(BF16)
    iq_ref[0] = (nproj(_N_IQ, _N_IW) * (IDX_DIM ** -0.5)).astype(BF16)
    iw_ref[0] = nproj(_N_IW, _N_END) * (IDX_HEADS ** -0.5)
    gvT_ref[0] = tproj(_T_GV, _T_DK).astype(BF16)
    dkT_ref[0, 0] = tproj(_T_DK, _T_IK).astype(BF16)
    ikT_ref[0, 0] = tproj(_T_IK, _T_END).astype(BF16)


def _in_proj(x1, w_in, w_a2, b_a):
    bsz, t, _ = x1.shape
    o = [0, 256, 512, 1024, 1536, 1552, 2064, 2576, 3088, 3600, 3664, 3672]
    col = lambda i: w_in[:, o[i]:o[i + 1]]
    pad = lambda w: jnp.pad(w, ((0, 0), (0, LANES - w.shape[1])))
    wn = jnp.concatenate(
        [col(0), col(1), col(2), col(3), pad(col(4)), col(5), col(7), col(8), pad(col(10))],
        axis=1).astype(BF16)
    wt = jnp.concatenate([col(2), col(6), col(9), col(9)], axis=1).T.astype(BF16)
    wa2 = jnp.pad(w_a2, ((0, LANES - GLA_LOWRANK), (0, 0))).astype(BF16)
    ba = b_a.reshape(1, GLA_QK)
    nt = t // PROJ_TM
    tm = PROJ_TM
    row = lambda w: pl.BlockSpec((1, tm, w), lambda b, i: (b, i, 0))
    const2 = lambda b, i: (0, 0)
    out_shape = (
        jax.ShapeDtypeStruct((bsz, t, GLA_QK), F32),
        jax.ShapeDtypeStruct((bsz, t, GLA_QK), F32),
        jax.ShapeDtypeStruct((bsz, t, GLA_QK), F32),
        jax.ShapeDtypeStruct((bsz, t, GLA_WIDTH), BF16),
        jax.ShapeDtypeStruct((bsz, t, GLA_WIDTH), F32),
        jax.ShapeDtypeStruct((bsz, t, DSA_WIDTH), BF16),
        jax.ShapeDtypeStruct((bsz, t, DSA_WIDTH), BF16),
        jax.ShapeDtypeStruct((bsz, t, IDX_HEADS * IDX_DIM), BF16),
        jax.ShapeDtypeStruct((bsz, t, LANES), F32),
        jax.ShapeDtypeStruct((bsz, GLA_WIDTH, t), BF16),
        jax.ShapeDtypeStruct((bsz, nt, DSA_WIDTH, tm), BF16),
        jax.ShapeDtypeStruct((bsz, nt, 2 * IDX_DIM, tm), BF16),
    )
    out_specs = (
        row(GLA_QK), row(GLA_QK), row(GLA_QK), row(GLA_WIDTH), row(GLA_WIDTH),
        row(DSA_WIDTH), row(DSA_WIDTH), row(IDX_HEADS * IDX_DIM), row(LANES),
        pl.BlockSpec((1, GLA_WIDTH, tm), lambda b, i: (b, 0, i)),
        pl.BlockSpec((1, 1, DSA_WIDTH, tm), lambda b, i: (b, i, 0, 0)),
        pl.BlockSpec((1, 1, 2 * IDX_DIM, tm), lambda b, i: (b, i, 0, 0)),
    )
    return pl.pallas_call(
        _proj_kernel,
        name="in_proj",
        grid=(bsz, nt),
        in_specs=[
            pl.BlockSpec((1, tm, D_MODEL), lambda b, i: (b, i, 0)),
            pl.BlockSpec((D_MODEL, _N_END), const2, pipeline_mode=pl.Buffered(1)),
            pl.BlockSpec((_T_END, D_MODEL), const2, pipeline_mode=pl.Buffered(1)),
            pl.BlockSpec((LANES, GLA_QK), const2),
            pl.BlockSpec((1, GLA_QK), const2),
        ],
        out_specs=out_specs,
        out_shape=out_shape,
        compiler_params=pltpu.CompilerParams(
            dimension_semantics=("arbitrary", "arbitrary"), vmem_limit_bytes=VMEM_LIMIT_BYTES),
    )(x1, wn, wt, wa2, ba)


def _gla_kernel(q_ref, k_ref, la_ref, v_ref, vT_ref, gg_ref, g_ref, o_ref, s_ref):
    c = GLA_C

    @pl.when(pl.program_id(1) == 0)
    def _():
        s_ref[...] = jnp.zeros_like(s_ref)

    ri = lax.broadcasted_iota(jnp.int32, (c, c), 0)
    ci = lax.broadcasted_iota(jnp.int32, (c, c), 1)
    causal = ci <= ri
    tri = jnp.where(causal, 1.0, 0.0).astype(BF16)
    lane = lax.broadcasted_iota(jnp.int32, (c, LANES), 1)
    halves = (lane < GLA_DK, lane >= GLA_DK)
    gnorm = g_ref[...]

    for ch in range(GLA_TT // c):
        rows = slice(ch * c, (ch + 1) * c)
        la = la_ref[0, rows, :]
        la_hi = la.astype(BF16)
        la_lo = (la - la_hi.astype(F32)).astype(BF16)
        b = (jnp.dot(tri, la_hi, preferred_element_type=F32)
             + jnp.dot(tri, la_lo, preferred_element_type=F32))
        b_last = b[c - 1:c, :]
        q = q_ref[0, rows, :] * (GLA_DK ** -0.5)
        k = k_ref[0, rows, :]
        qd = (q * jnp.exp(b)).astype(BF16)
        kd = (k * jnp.exp(-b)).astype(BF16)
        kdl = (k * jnp.exp(b_last - b)).astype(BF16)
        dec = jnp.exp(b_last)
        for h in range(GLA_HEADS):
            p, hh = divmod(h, 2)
            pl_ = slice(p * LANES, (p + 1) * LANES)
            qd_p = qd[:, pl_]
            kd_m = jnp.where(halves[hh], kd[:, pl_], 0.0).astype(BF16)
            kdl_m = jnp.where(halves[hh], kdl[:, pl_], 0.0).astype(BF16)
            a = lax.dot_general(qd_p, kd_m, _NT, preferred_element_type=F32)
            a = jnp.where(causal, a, 0.0).astype(BF16)
            hl = slice(h * GLA_DV, (h + 1) * GLA_DV)
            o = jnp.dot(a, v_ref[0, rows, hl], preferred_element_type=F32)
            st = s_ref[h]
            o = o + lax.dot_general(qd_p, st.astype(BF16), _NT, preferred_element_type=F32)
            s_ref[h] = st * dec[:, pl_] + jnp.dot(
                vT_ref[0, hl, rows], kdl_m, preferred_element_type=F32)
            ms = jnp.mean(o * o, axis=-1, keepdims=True)
            on = o * lax.rsqrt(ms + RMS_EPS) * gnorm
            o_ref[0, rows, hl] = (on * _silu(gg_ref[0, rows, hl])).astype(o_ref.dtype)


def _gla(gq, gk, la, gv, gvT, gg, gnorm):
    bsz, t, _ = gq.shape
    tt = GLA_TT
    row = lambda w: pl.BlockSpec((1, tt, w), lambda b, i: (b, i, 0))
    return pl.pallas_call(
        _gla_kernel,
        name="gla",
        grid=(bsz, t // tt),
        in_specs=[
            row(GLA_QK), row(GLA_QK), row(GLA_QK), row(GLA_WIDTH),
            pl.BlockSpec((1, GLA_WIDTH, tt), lambda b, i: (b, 0, i)),
            row(GLA_WIDTH),
            pl.BlockSpec((1, GLA_DV), lambda b, i: (0, 0)),
        ],
        out_specs=row(GLA_WIDTH),
        out_shape=jax.ShapeDtypeStruct((bsz, t, GLA_WIDTH), BF16),
        scratch_shapes=[pltpu.VMEM((GLA_HEADS, GLA_DV, LANES), F32)],
        compiler_params=pltpu.CompilerParams(
            dimension_semantics=("arbitrary", "arbitrary"), vmem_limit_bytes=VMEM_LIMIT_BYTES),
    )(gq, gk, la, gv, gvT, gg, gnorm.reshape(1, GLA_DV))


def _dsa_kernel(q_ref, qi_ref, w_ref, kT_ref, v_ref, kiT_ref, o_ref, key_ref, bias_ref, *, topk):
    tq, tk = DSA_TQ, DSA_TK
    j = pl.program_id(1)
    q0 = j * tq
    nt = (q0 + tq + tk - 1) // tk

    lane = lax.broadcasted_iota(jnp.int32, (tq, LANES), 1)
    halves = (lane < DSA_DH, lane >= DSA_DH)
    rowi = lax.broadcasted_iota(jnp.int32, (tq, 1), 0)
    lim = ((q0 + rowi) // CHUNK + 1) * CHUNK
    kcol = lax.broadcasted_iota(jnp.int32, (tq, tk), 1)

    qi = qi_ref[0]
    w = w_ref[0]

    def score_tile(t, carry):
        kit = kiT_ref[0, t]
        acc = jnp.zeros((tq, tk), F32)
        for h in range(IDX_HEADS):
            p, hh = divmod(h, 2)
            qm = jnp.where(halves[hh], qi[:, p * LANES:(p + 1) * LANES], 0.0).astype(BF16)
            s = jnp.dot(qm, kit, preferred_element_type=F32)
            acc = acc + w[:, h:h + 1] * jnp.maximum(s, 0.0)
        bits = pltpu.bitcast(acc, jnp.int32)
        sk = bits ^ ((bits >> 31) & 0x7FFFFFFF)
        sk = jnp.where(sk == -1, 0, sk)
        sk = jnp.where(kcol + t * tk < lim, sk, INT_MIN)
        key_ref[t] = sk
        return carry

    lax.fori_loop(0, nt, score_tile, 0)

    def count_ge(cand):
        def tile(t, acc):
            ge = key_ref[t] >= cand
            for g in range(tk // LANES):
                acc = acc + jnp.where(ge[:, g * LANES:(g + 1) * LANES], 1.0, 0.0)
            return acc
        acc = lax.fori_loop(0, nt, tile, jnp.zeros((tq, LANES), F32))
        return jnp.sum(acc, axis=1, keepdims=True)

    def bit_body(i, thr):
        cand = thr + lax.shift_left(jnp.int32(1), 31 - i)
        return jnp.where(count_ge(cand) >= topk, cand, thr)

    thr = lax.fori_loop(0, 32, bit_body, jnp.full((tq, 1), INT_MIN, jnp.int32))
    thr = jnp.maximum(thr, INT_MIN + 1)
    n_ge = count_ge(thr)
    n_gt = count_ge(thr + 1)
    has_tie = jnp.max(n_ge) > topk

    @pl.when(jnp.logical_not(has_tie))
    def _():
        def tile(t, carry):
            bias_ref[t] = jnp.where(key_ref[t] >= thr, 0.0, NEG_BIG)
            return carry
        lax.fori_loop(0, nt, tile, 0)

    @pl.when(has_tie)
    def _():
        need = topk - n_gt

        def count_eq_below(pos):
            def tile(t, acc):
                hit = jnp.logical_and(key_ref[t] == thr, kcol + t * tk < pos)
                for g in range(tk // LANES):
                    acc = acc + jnp.where(hit[:, g * LANES:(g + 1) * LANES], 1.0, 0.0)
                return acc
            acc = lax.fori_loop(0, nt, tile, jnp.zeros((tq, LANES), F32))
            return jnp.sum(acc, axis=1, keepdims=True)

        def pos_body(i, pos):
            cand = pos + lax.shift_left(jnp.int32(1), 12 - i)
            return jnp.where(count_eq_below(cand) < need, cand, pos)

        pos = lax.fori_loop(0, 13, pos_body, jnp.zeros((tq, 1), jnp.int32))

        def tile(t, carry):
            sk = key_ref[t]
            keep = jnp.logical_or(sk > thr, jnp.logical_and(sk == thr, kcol + t * tk <= pos))
            bias_ref[t] = jnp.where(keep, 0.0, NEG_BIG)
            return carry
        lax.fori_loop(0, nt, tile, 0)

    q = q_ref[0]
    for p in range(DSA_HEADS // 2):
        pl_ = slice(p * LANES, (p + 1) * LANES)
        outs = []
        for hh in range(2):
            qm = jnp.where(halves[hh], q[:, pl_], 0.0).astype(BF16)

            def tile(t, carry, qm=qm, pl_=pl_):
                m, l, acc = carry
                s = jnp.dot(qm, kT_ref[0, t, pl_, :], preferred_element_type=F32) + bias_ref[t]
                m_new = jnp.maximum(m, jnp.max(s, axis=1, keepdims=True))
                alpha = jnp.exp(m - m_new)
                pr = jnp.exp(s - m_new)
                l = alpha * l + jnp.sum(pr, axis=1, keepdims=True)
                vt = v_ref[0, pl.ds(pl.multiple_of(t * tk, tk), tk), pl_]
                acc = alpha * acc + jnp.dot(pr.astype(BF16), vt, preferred_element_type=F32)
                return m_new, l, acc

            m, l, acc = lax.fori_loop(
                0, nt, tile,
                (jnp.full((tq, 1), NEG_BIG, F32), jnp.zeros((tq, 1), F32),
                 jnp.zeros((tq, LANES), F32)))
            outs.append(acc / l)
        o_ref[0, :, pl_] = jnp.where(halves[0], outs[0], outs[1]).astype(o_ref.dtype)


def _dsa(dq, iq, iw, dkT, dv, ikT):
    bsz, t, _ = dq.shape
    tq, tk = DSA_TQ, DSA_TK
    nt = t // tk
    topk = min(TOPK_MAX, t // 4)
    row = lambda w: pl.BlockSpec((1, tq, w), lambda b, i: (b, i, 0))
    return pl.pallas_call(
        functools.partial(_dsa_kernel, topk=topk),
        name="dsa",
        grid=(bsz, t // tq),
        in_specs=[
            row(DSA_WIDTH), row(IDX_HEADS * IDX_DIM), row(LANES),
            pl.BlockSpec((1, nt, DSA_WIDTH, tk), lambda b, i: (b, 0, 0, 0)),
            pl.BlockSpec((1, t, DSA_WIDTH), lambda b, i: (b, 0, 0)),
            pl.BlockSpec((1, nt, 2 * IDX_DIM, tk), lambda b, i: (b, 0, 0, 0)),
        ],
        out_specs=row(DSA_WIDTH),
        out_shape=jax.ShapeDtypeStruct((bsz, t, DSA_WIDTH), BF16),
        scratch_shapes=[
            pltpu.VMEM((nt, tq, tk), jnp.int32),
            pltpu.VMEM((nt, tq, tk), F32),
        ],
        compiler_params=pltpu.CompilerParams(
            dimension_semantics=("arbitrary", "arbitrary"), vmem_limit_bytes=VMEM_LIMIT_BYTES),
    )(dq, iq, iw, dkT, dv, ikT)


def _out_ln_kernel(x_ref, og_ref, od_ref, wg_ref, wd_ref, g_ref, b_ref, o_ref):
    mix = (jnp.dot(og_ref[...], wg_ref[...], preferred_element_type=F32)
           + jnp.dot(od_ref[...], wd_ref[...], preferred_element_type=F32))
    y = DEEPNORM_ALPHA * x_ref[...] + mix
    o_ref[...] = _layer_norm(y, g_ref[...], b_ref[...])


def _out_ln(x2d, og, od, w_out, g, b):
    n = x2d.shape[0]
    tm = FFN_TM
    wg = w_out[:GLA_WIDTH].astype(BF16)
    wd = w_out[GLA_WIDTH:].astype(BF16)
    const = lambda i: (0, 0)
    return pl.pallas_call(
        _out_ln_kernel,
        name="out_ln",
        grid=(n // tm,),
        in_specs=[
            pl.BlockSpec((tm, D_MODEL), lambda i: (i, 0)),
            pl.BlockSpec((tm, GLA_WIDTH), lambda i: (i, 0)),
            pl.BlockSpec((tm, DSA_WIDTH), lambda i: (i, 0)),
            pl.BlockSpec((GLA_WIDTH, D_MODEL), const),
            pl.BlockSpec((DSA_WIDTH, D_MODEL), const),
            pl.BlockSpec((1, D_MODEL), const),
            pl.BlockSpec((1, D_MODEL), const),
        ],
        out_specs=pl.BlockSpec((tm, D_MODEL), lambda i: (i, 0)),
        out_shape=jax.ShapeDtypeStruct((n, D_MODEL), F32),
        compiler_params=pltpu.CompilerParams(
            dimension_semantics=("arbitrary",), vmem_limit_bytes=VMEM_LIMIT_BYTES),
    )(x2d, og, od, wg, wd, g.reshape(1, D_MODEL), b.reshape(1, D_MODEL))


def kernel(x, w_in, w_gla_a2, b_gla_a, gla_norm_g, w_out, ffn1_w_gu, ffn1_w_down,
           ffn2_w_gu, ffn2_w_down, ln1_g, ln1_b, ln2_g, ln2_b, ln3_g, ln3_b):
    bsz, t, d = x.shape
    n = bsz * t
    for l in range(DEPTH):
        x1 = _ffn_ln(x.reshape(n, d), ffn1_w_gu[l], ffn1_w_down[l], ln1_g[l], ln1_b[l])
        gq, gk, la, gv, gg, dq, dv, iq, iw, gvT, dkT, ikT = _in_proj(
            x1.reshape(bsz, t, d), w_in[l], w_gla_a2[l], b_gla_a[l])
        o_gla = _gla(gq, gk, la, gv, gvT, gg, gla_norm_g[l])
        o_dsa = _dsa(dq, iq, iw, dkT, dv, ikT)
        x2 = _out_ln(x1, o_gla.reshape(n, GLA_WIDTH), o_dsa.reshape(n, DSA_WIDTH),
                     w_out[l], ln2_g[l], ln2_b[l])
        x = _ffn_ln(x2, ffn2_w_gu[l], ffn2_w_down[l], ln3_g[l], ln3_b[l]).reshape(bsz, t, d)
    return x
```

```python
import functools

import jax
import jax.numpy as jnp
from jax import lax
from jax.experimental import pallas as pl
from jax.experimental.pallas import tpu as pltpu

F32 = jnp.float32
BF16 = jnp.bfloat16

D_MODEL = 1024
D_FF = 2816
CHUNK = 64
GLA_HEADS = 4
GLA_DK = 64
GLA_DV = 128
GLA_QK = GLA_HEADS * GLA_DK
GLA_WIDTH = GLA_HEADS * GLA_DV
GLA_LOWRANK = 16
GLA_TAU = 16.0
DSA_HEADS = 8
DSA_DH = 64
DSA_WIDTH = DSA_HEADS * DSA_DH
IDX_HEADS = 8
IDX_DIM = 64
TOPK_MAX = 256
LN_EPS = 1e-5
RMS_EPS = 1e-6
DEPTH = 1
DEEPNORM_ALPHA = (2.0 * DEPTH) ** 0.25

LANES = 128
SUBLANES = 8
VMEM_LIMIT_BYTES = 56 * 1024 * 1024

FFN_TM = 512
FFN_FC = 256
PROJ_TM = 512
GLA_TT = 512
GLA_C = 128
DSA_TQ = 256
DSA_TK = 256
NEG_BIG = -1e30
INT_MIN = -(2 ** 31)
HALF_BITS = 16
HALF_MASK = (1 << HALF_BITS) - 1
I16_MIN = -(2 ** 15)
PACKED_SUBLANES = 16

_NT = (((1,), (1,)), ((), ()))


def _layer_norm(y, g, b):
    mu = jnp.mean(y, axis=-1, keepdims=True)
    yc = y - mu
    var = jnp.mean(yc * yc, axis=-1, keepdims=True)
    return yc * lax.rsqrt(var + LN_EPS) * g + b


def _silu(x):
    return x * jax.nn.sigmoid(x)


def _ffn_ln_kernel(x_ref, wg_ref, wu_ref, wd_ref, g_ref, b_ref, o_ref, xb_ref, acc_ref):
    x = x_ref[...]
    xb_ref[...] = x.astype(BF16)
    acc_ref[...] = jnp.zeros_like(acc_ref)

    def body(c, carry):
        xb = xb_ref[...]
        g = jnp.dot(xb, wg_ref[c], preferred_element_type=F32)
        u = jnp.dot(xb, wu_ref[c], preferred_element_type=F32)
        a = (_silu(g) * u).astype(BF16)
        acc_ref[...] += jnp.dot(a, wd_ref[c], preferred_element_type=F32)
        return carry

    lax.fori_loop(0, D_FF // FFN_FC, body, 0)
    y = DEEPNORM_ALPHA * x + 0.5 * acc_ref[...]
    o_ref[...] = _layer_norm(y, g_ref[...], b_ref[...])


def _ffn_ln(x2d, w_gu, w_down, g, b):
    n = x2d.shape[0]
    nc = D_FF // FFN_FC
    wg = w_gu[:, :D_FF].astype(BF16).reshape(D_MODEL, nc, FFN_FC).transpose(1, 0, 2)
    wu = w_gu[:, D_FF:].astype(BF16).reshape(D_MODEL, nc, FFN_FC).transpose(1, 0, 2)
    wd = w_down.astype(BF16).reshape(nc, FFN_FC, D_MODEL)
    full3 = lambda i: (0, 0, 0)
    return pl.pallas_call(
        _ffn_ln_kernel,
        name="ffn_ln",
        grid=(n // FFN_TM,),
        in_specs=[
            pl.BlockSpec((FFN_TM, D_MODEL), lambda i: (i, 0)),
            pl.BlockSpec((nc, D_MODEL, FFN_FC), full3, pipeline_mode=pl.Buffered(1)),
            pl.BlockSpec((nc, D_MODEL, FFN_FC), full3, pipeline_mode=pl.Buffered(1)),
            pl.BlockSpec((nc, FFN_FC, D_MODEL), full3, pipeline_mode=pl.Buffered(1)),
            pl.BlockSpec((1, D_MODEL), lambda i: (0, 0)),
            pl.BlockSpec((1, D_MODEL), lambda i: (0, 0)),
        ],
        out_specs=pl.BlockSpec((FFN_TM, D_MODEL), lambda i: (i, 0)),
        out_shape=jax.ShapeDtypeStruct((n, D_MODEL), F32),
        scratch_shapes=[
            pltpu.VMEM((FFN_TM, D_MODEL), BF16),
            pltpu.VMEM((FFN_TM, D_MODEL), F32),
        ],
        compiler_params=pltpu.CompilerParams(
            dimension_semantics=("arbitrary",), vmem_limit_bytes=VMEM_LIMIT_BYTES),
    )(x2d, wg, wu, wd, g.reshape(1, D_MODEL), b.reshape(1, D_MODEL))


_N_GQ, _N_GK, _N_GV, _N_GG, _N_GA, _N_DQ, _N_DK, _N_IQ, _N_IK, _N_END = (
    0, 256, 512, 1024, 1536, 1664, 2176, 2688, 3200, 3328)
_T_GV, _T_DV, _T_IW, _T_END = 0, 512, 1024, 1040


def _proj_kernel(x_ref, wn_ref, wt_ref, wa2_ref, ba_ref,
                 gq_ref, gk_ref, la_ref, gv_ref, gg_ref, dq_ref, dk_ref, iq_ref, ik_ref,
                 gvT_ref, dvT_ref, iwT_ref):
    xb = x_ref[0].astype(BF16)

    def nproj(lo, hi):
        return jnp.dot(xb, wn_ref[:, lo:hi], preferred_element_type=F32)

    def tproj(lo, hi):
        return lax.dot_general(wt_ref[lo:hi, :], xb, _NT, preferred_element_type=F32)

    gq_ref[0] = nproj(_N_GQ, _N_GK)
    gk_ref[0] = nproj(_N_GK, _N_GV)
    gv_ref[0] = nproj(_N_GV, _N_GG).astype(BF16)
    gg_ref[0] = nproj(_N_GG, _N_GA)
    ga = nproj(_N_GA, _N_DQ)
    z = jnp.dot(ga.astype(BF16), wa2_ref[...], preferred_element_type=F32) + ba_ref[...]
    la_ref[0] = jax.nn.log_sigmoid(z) * (1.0 / GLA_TAU)
    dq_ref[0] = (nproj(_N_DQ, _N_DK) * (DSA_DH ** -0.5)).astype(BF16)
    dk_ref[0] = nproj(_N_DK, _N_IQ).astype(BF16)
    iq_ref[0] = (nproj(_N_IQ, _N_IK) * (IDX_DIM ** -0.5)).astype(BF16)
    ik_ref[0] = nproj(_N_IK, _N_END).astype(BF16)
    gvT_ref[0] = tproj(_T_GV, _T_DV).astype(BF16)
    dvT = tproj(_T_DV, _T_IW).astype(BF16)
    for s in range(PROJ_TM // DSA_TK):
        dvT_ref[0, s] = dvT[:, s * DSA_TK:(s + 1) * DSA_TK]
    iwT_ref[0] = tproj(_T_IW, _T_END)[:IDX_HEADS, :] * (IDX_HEADS ** -0.5)


def _in_proj(x1, w_in, w_a2, b_a):
    bsz, t, _ = x1.shape
    o = [0, 256, 512, 1024, 1536, 1552, 2064, 2576, 3088, 3600, 3664, 3672]
    col = lambda i: w_in[:, o[i]:o[i + 1]]
    pad = lambda w, n: jnp.pad(w, ((0, 0), (0, n - w.shape[1])))
    wn = jnp.concatenate(
        [col(0), col(1), col(2), col(3), pad(col(4), LANES), col(5), col(6), col(8), col(9), col(9)],
        axis=1).astype(BF16)
    wt = jnp.concatenate([col(2), col(7), pad(col(10), 16)], axis=1).T.astype(BF16)
    wa2 = jnp.pad(w_a2, ((0, LANES - GLA_LOWRANK), (0, 0))).astype(BF16)
    ba = b_a.reshape(1, GLA_QK)
    tm = PROJ_TM
    spt = tm // DSA_TK
    row = lambda w: pl.BlockSpec((1, tm, w), lambda b, i: (b, i, 0))
    const2 = lambda b, i: (0, 0)
    out_shape = (
        jax.ShapeDtypeStruct((bsz, t, GLA_QK), F32),
        jax.ShapeDtypeStruct((bsz, t, GLA_QK), F32),
        jax.ShapeDtypeStruct((bsz, t, GLA_QK), F32),
        jax.ShapeDtypeStruct((bsz, t, GLA_WIDTH), BF16),
        jax.ShapeDtypeStruct((bsz, t, GLA_WIDTH), F32),
        jax.ShapeDtypeStruct((bsz, t, DSA_WIDTH), BF16),
        jax.ShapeDtypeStruct((bsz, t, DSA_WIDTH), BF16),
        jax.ShapeDtypeStruct((bsz, t, IDX_HEADS * IDX_DIM), BF16),
        jax.ShapeDtypeStruct((bsz, t, 2 * IDX_DIM), BF16),
        jax.ShapeDtypeStruct((bsz, GLA_WIDTH, t), BF16),
        jax.ShapeDtypeStruct((bsz, t // DSA_TK, DSA_WIDTH, DSA_TK), BF16),
        jax.ShapeDtypeStruct((bsz, IDX_HEADS, t), F32),
    )
    out_specs = (
        row(GLA_QK), row(GLA_QK), row(GLA_QK), row(GLA_WIDTH), row(GLA_WIDTH),
        row(DSA_WIDTH), row(DSA_WIDTH), row(IDX_HEADS * IDX_DIM), row(2 * IDX_DIM),
        pl.BlockSpec((1, GLA_WIDTH, tm), lambda b, i: (b, 0, i)),
        pl.BlockSpec((1, spt, DSA_WIDTH, DSA_TK), lambda b, i: (b, i, 0, 0)),
        pl.BlockSpec((1, IDX_HEADS, tm), lambda b, i: (b, 0, i)),
    )
    return pl.pallas_call(
        _proj_kernel,
        name="in_proj",
        grid=(bsz, t // tm),
        in_specs=[
            pl.BlockSpec((1, tm, D_MODEL), lambda b, i: (b, i, 0)),
            pl.BlockSpec((D_MODEL, _N_END), const2, pipeline_mode=pl.Buffered(1)),
            pl.BlockSpec((_T_END, D_MODEL), const2, pipeline_mode=pl.Buffered(1)),
            pl.BlockSpec((LANES, GLA_QK), const2),
            pl.BlockSpec((1, GLA_QK), const2),
        ],
        out_specs=out_specs,
        out_shape=out_shape,
        compiler_params=pltpu.CompilerParams(
            dimension_semantics=("arbitrary", "arbitrary"), vmem_limit_bytes=VMEM_LIMIT_BYTES),
    )(x1, wn, wt, wa2, ba)


def _gla_kernel(q_ref, k_ref, la_ref, v_ref, vT_ref, gg_ref, g_ref, o_ref, s_ref):
    c = GLA_C

    @pl.when(pl.program_id(1) == 0)
    def _():
        s_ref[...] = jnp.zeros_like(s_ref)

    ri = lax.broadcasted_iota(jnp.int32, (c, c), 0)
    ci = lax.broadcasted_iota(jnp.int32, (c, c), 1)
    causal = ci <= ri
    tri = jnp.where(causal, 1.0, 0.0).astype(BF16)
    lane = lax.broadcasted_iota(jnp.int32, (c, LANES), 1)
    halves = (lane < GLA_DK, lane >= GLA_DK)
    gnorm = g_ref[...]

    for ch in range(GLA_TT // c):
        rows = slice(ch * c, (ch + 1) * c)
        la = la_ref[0, rows, :]
        la_hi = la.astype(BF16)
        la_lo = (la - la_hi.astype(F32)).astype(BF16)
        b = (jnp.dot(tri, la_hi, preferred_element_type=F32)
             + jnp.dot(tri, la_lo, preferred_element_type=F32))
        b_last = b[c - 1:c, :]
        q = q_ref[0, rows, :] * (GLA_DK ** -0.5)
        k = k_ref[0, rows, :]
        qd = (q * jnp.exp(b)).astype(BF16)
        kd = (k * jnp.exp(-b)).astype(BF16)
        kdl = (k * jnp.exp(b_last - b)).astype(BF16)
        dec = jnp.exp(b_last)
        for h in range(GLA_HEADS):
            p, hh = divmod(h, 2)
            pl_ = slice(p * LANES, (p + 1) * LANES)
            qd_p = qd[:, pl_]
            kd_m = jnp.where(halves[hh], kd[:, pl_], 0.0).astype(BF16)
            kdl_m = jnp.where(halves[hh], kdl[:, pl_], 0.0).astype(BF16)
            a = lax.dot_general(qd_p, kd_m, _NT, preferred_element_type=F32)
            a = jnp.where(causal, a, 0.0).astype(BF16)
            hl = slice(h * GLA_DV, (h + 1) * GLA_DV)
            o = jnp.dot(a, v_ref[0, rows, hl], preferred_element_type=F32)
            st = s_ref[h]
            o = o + lax.dot_general(qd_p, st.astype(BF16), _NT, preferred_element_type=F32)
            s_ref[h] = st * dec[:, pl_] + jnp.dot(
                vT_ref[0, hl, rows], kdl_m, preferred_element_type=F32)
            ms = jnp.mean(o * o, axis=-1, keepdims=True)
            on = o * lax.rsqrt(ms + RMS_EPS) * gnorm
            o_ref[0, rows, hl] = (on * _silu(gg_ref[0, rows, hl])).astype(o_ref.dtype)


def _gla(gq, gk, la, gv, gvT, gg, gnorm):
    bsz, t, _ = gq.shape
    tt = GLA_TT
    row = lambda w: pl.BlockSpec((1, tt, w), lambda b, i: (b, i, 0))
    return pl.pallas_call(
        _gla_kernel,
        name="gla",
        grid=(bsz, t // tt),
        in_specs=[
            row(GLA_QK), row(GLA_QK), row(GLA_QK), row(GLA_WIDTH),
            pl.BlockSpec((1, GLA_WIDTH, tt), lambda b, i: (b, 0, i)),
            row(GLA_WIDTH),
            pl.BlockSpec((1, GLA_DV), lambda b, i: (0, 0)),
        ],
        out_specs=row(GLA_WIDTH),
        out_shape=jax.ShapeDtypeStruct((bsz, t, GLA_WIDTH), BF16),
        scratch_shapes=[pltpu.VMEM((GLA_HEADS, GLA_DV, LANES), F32)],
        compiler_params=pltpu.CompilerParams(
            dimension_semantics=("arbitrary", "arbitrary"), vmem_limit_bytes=VMEM_LIMIT_BYTES),
    )(gq, gk, la, gv, gvT, gg, gnorm.reshape(1, GLA_DV))


def _dsa_kernel(q_ref, qi_ref, wT_ref, k_ref, vT_ref, ki_ref, oT_ref,
                key_ref, hi_ref, lo_ref, bias_ref, qm_ref, qim_ref, s_ref, p_ref,
                m_ref, l_ref, al_ref, acc_ref, *, topk):
    tq, tk = DSA_TQ, DSA_TK
    j = pl.program_id(1)
    nt = j + 1
    q0 = j * tq

    lane = lax.broadcasted_iota(jnp.int32, (tq, LANES), 1)
    halves = (lane < DSA_DH, lane >= DSA_DH)
    qcol = q0 + lax.broadcasted_iota(jnp.int32, (1, tq), 1)
    lim = (qcol // CHUNK + 1) * CHUNK
    krow = lax.broadcasted_iota(jnp.int32, (tk, tq), 0)

    for h in range(DSA_HEADS):
        p, hh = divmod(h, 2)
        pl_ = slice(p * LANES, (p + 1) * LANES)
        qm_ref[h] = jnp.where(halves[hh], q_ref[0, :, pl_], 0.0).astype(BF16)
        qim_ref[h] = jnp.where(halves[hh], qi_ref[0, :, pl_], 0.0).astype(BF16)

    wT = wT_ref[0]

    def score_tile(t, carry):
        r0 = pl.multiple_of(t * tk, tk)
        ki2 = ki_ref[0, pl.ds(r0, tk), :]
        for h in range(IDX_HEADS):
            s_ref[h] = lax.dot_general(ki2, qim_ref[h], _NT, preferred_element_type=F32)
        acc = jnp.zeros((tk, tq), F32)
        for h in range(IDX_HEADS):
            acc = acc + wT[h:h + 1, :] * jnp.maximum(s_ref[h], 0.0)
        bits = pltpu.bitcast(acc, jnp.int32)
        sk = bits ^ ((bits >> 31) & 0x7FFFFFFF)
        sk = jnp.where(sk == -1, 0, sk)
        sk = jnp.where(krow + r0 < lim, sk, INT_MIN)
        key_ref[t] = sk
        hi_ref[t] = (sk >> HALF_BITS).astype(jnp.int16)
        lo_ref[t] = ((sk & HALF_MASK) + I16_MIN).astype(jnp.int16)
        return carry

    lax.fori_loop(0, nt, score_tile, 0)

    def count16(ref, cand):
        cand16 = cand.astype(jnp.int16)

        def tile(t, acc):
            hit = jnp.where(ref[t] >= cand16, jnp.int16(1), jnp.int16(0))
            for g in range(tk // PACKED_SUBLANES):
                acc = acc + hit[g * PACKED_SUBLANES:(g + 1) * PACKED_SUBLANES, :]
            return acc
        acc = lax.fori_loop(0, nt, tile, jnp.zeros((PACKED_SUBLANES, tq), jnp.int16))
        return jnp.sum(acc.astype(jnp.int32), axis=0, keepdims=True)

    def hi_bit(i, thi):
        cand = thi + lax.shift_left(jnp.int32(1), HALF_BITS - 1 - i)
        return jnp.where(count16(hi_ref, cand) >= topk, cand, thi)

    thi = lax.fori_loop(0, HALF_BITS, hi_bit, jnp.full((1, tq), I16_MIN, jnp.int32))
    n_above = jnp.where(thi < -I16_MIN - 1, count16(hi_ref, jnp.minimum(thi + 1, -I16_MIN - 1)), 0)
    need_lo = topk - n_above

    thi16 = thi.astype(jnp.int16)

    def mask_lo(t, carry):
        lo_ref[t] = jnp.where(hi_ref[t] == thi16, lo_ref[t], jnp.int16(I16_MIN))
        return carry

    lax.fori_loop(0, nt, mask_lo, 0)

    def lo_bit(i, state):
        tlo, cnt = state
        cand = tlo + lax.shift_left(jnp.int32(1), HALF_BITS - 1 - i)
        c = count16(lo_ref, cand)
        take = c >= need_lo
        return jnp.where(take, cand, tlo), jnp.where(take, c, cnt)

    tlo, cnt = lax.fori_loop(
        0, HALF_BITS, lo_bit,
        (jnp.full((1, tq), I16_MIN, jnp.int32), jnp.full((1, tq), -1, jnp.int32)))
    has_tie = jnp.max(jnp.where(jnp.logical_and(cnt != need_lo, lim > topk), 1, 0))
    thr = thi * (1 << HALF_BITS) + (tlo - I16_MIN)
    thr = jnp.where(lim > topk, jnp.maximum(thr, INT_MIN + 1), INT_MIN + 1)

    @pl.when(has_tie == 0)
    def _():
        def tile(t, carry):
            bias_ref[t] = jnp.where(key_ref[t] >= thr, 0.0, NEG_BIG)
            return carry
        lax.fori_loop(0, nt, tile, 0)

    @pl.when(has_tie > 0)
    def _():
        def gt_tile(t, acc):
            hit = jnp.where(key_ref[t] > thr, 1.0, 0.0)
            for g in range(tk // SUBLANES):
                acc = acc + hit[g * SUBLANES:(g + 1) * SUBLANES, :]
            return acc
        n_gt = jnp.sum(lax.fori_loop(0, nt, gt_tile, jnp.zeros((SUBLANES, tq), F32)),
                       axis=0, keepdims=True)
        need = topk - n_gt
        ri = lax.broadcasted_iota(jnp.int32, (tk, tk), 0)
        ci = lax.broadcasted_iota(jnp.int32, (tk, tk), 1)
        tri = jnp.where(ci <= ri, 1.0, 0.0).astype(BF16)

        def tile(t, before):
            sk = key_ref[t]
            eq = sk == thr
            rank = before + jnp.dot(tri, jnp.where(eq, 1.0, 0.0).astype(BF16),
                                    preferred_element_type=F32)
            tie_bias = jnp.where(rank <= need, 0.0, NEG_BIG)
            bias_ref[t] = jnp.where(sk > thr, 0.0, jnp.where(eq, tie_bias, NEG_BIG))
            return rank[tk - 1:tk, :]
        lax.fori_loop(0, nt, tile, jnp.zeros((1, tq), F32))

    m_ref[...] = jnp.full(m_ref.shape, NEG_BIG, F32)
    l_ref[...] = jnp.zeros(l_ref.shape, F32)
    acc_ref[...] = jnp.zeros(acc_ref.shape, F32)

    def att_tile(t, carry):
        r0 = pl.multiple_of(t * tk, tk)
        for h in range(DSA_HEADS):
            p = h // 2
            kt = k_ref[0, pl.ds(r0, tk), p * LANES:(p + 1) * LANES]
            s_ref[h] = lax.dot_general(kt, qm_ref[h], _NT, preferred_element_type=F32)
        bias = bias_ref[t]
        for h in range(DSA_HEADS):
            s = s_ref[h] + bias
            m_old = m_ref[h:h + 1, :]
            m_new = jnp.maximum(m_old, jnp.max(s, axis=0, keepdims=True))
            alpha = jnp.exp(m_old - m_new)
            pr = jnp.exp(s - m_new)
            p_ref[h] = pr.astype(BF16)
            l_ref[h:h + 1, :] = alpha * l_ref[h:h + 1, :] + jnp.sum(pr, axis=0, keepdims=True)
            m_ref[h:h + 1, :] = m_new
            al_ref[h:h + 1, :] = alpha
        for h in range(DSA_HEADS):
            vt = vT_ref[0, t, h * DSA_DH:(h + 1) * DSA_DH, :]
            acc_ref[h] = al_ref[h:h + 1, :] * acc_ref[h] + jnp.dot(
                vt, p_ref[h], preferred_element_type=F32)
        return carry

    lax.fori_loop(0, nt, att_tile, 0)
    for h in range(DSA_HEADS):
        oT_ref[0, h * DSA_DH:(h + 1) * DSA_DH, :] = (
            acc_ref[h] / l_ref[h:h + 1, :]).astype(oT_ref.dtype)


def _dsa(dq, iq, iwT, dk, dvT, ik2):
    bsz, t, _ = dq.shape
    tq, tk = DSA_TQ, DSA_TK
    assert tq == tk and t % tq == 0
    nt = t // tk
    assert nt * (tk // PACKED_SUBLANES) < -I16_MIN
    topk = min(TOPK_MAX, t // 4)
    row = lambda w: pl.BlockSpec((1, tq, w), lambda b, i: (b, i, 0))
    return pl.pallas_call(
        functools.partial(_dsa_kernel, topk=topk),
        name="dsa",
        grid=(bsz, t // tq),
        in_specs=[
            row(DSA_WIDTH), row(IDX_HEADS * IDX_DIM),
            pl.BlockSpec((1, IDX_HEADS, tq), lambda b, i: (b, 0, i)),
            pl.BlockSpec((1, t, DSA_WIDTH), lambda b, i: (b, 0, 0)),
            pl.BlockSpec((1, nt, DSA_WIDTH, tk), lambda b, i: (b, 0, 0, 0)),
            pl.BlockSpec((1, t, 2 * IDX_DIM), lambda b, i: (b, 0, 0)),
        ],
        out_specs=pl.BlockSpec((1, DSA_WIDTH, tq), lambda b, i: (b, 0, i)),
        out_shape=jax.ShapeDtypeStruct((bsz, DSA_WIDTH, t), BF16),
        scratch_shapes=[
            pltpu.VMEM((nt, tk, tq), jnp.int32),
            pltpu.VMEM((nt, tk, tq), jnp.int16),
            pltpu.VMEM((nt, tk, tq), jnp.int16),
            pltpu.VMEM((nt, tk, tq), F32),
            pltpu.VMEM((DSA_HEADS, tq, LANES), BF16),
            pltpu.VMEM((IDX_HEADS, tq, LANES), BF16),
            pltpu.VMEM((DSA_HEADS, tk, tq), F32),
            pltpu.VMEM((DSA_HEADS, tk, tq), BF16),
            pltpu.VMEM((DSA_HEADS, tq), F32),
            pltpu.VMEM((DSA_HEADS, tq), F32),
            pltpu.VMEM((DSA_HEADS, tq), F32),
            pltpu.VMEM((DSA_HEADS, DSA_DH, tq), F32),
        ],
        compiler_params=pltpu.CompilerParams(
            dimension_semantics=("arbitrary", "arbitrary"), vmem_limit_bytes=VMEM_LIMIT_BYTES),
    )(dq, iq, iwT, dk, dvT, ik2)


def _out_ln_kernel(x_ref, og_ref, od_ref, wg_ref, wd_ref, g_ref, b_ref, o_ref):
    mix = (jnp.dot(og_ref[...], wg_ref[...], preferred_element_type=F32)
           + jnp.dot(od_ref[...], wd_ref[...], preferred_element_type=F32))
    y = DEEPNORM_ALPHA * x_ref[...] + mix
    o_ref[...] = _layer_norm(y, g_ref[...], b_ref[...])


def _out_ln(x2d, og, od, w_out, g, b):
    n = x2d.shape[0]
    tm = FFN_TM
    wg = w_out[:GLA_WIDTH].astype(BF16)
    wd = w_out[GLA_WIDTH:].astype(BF16)
    const = lambda i: (0, 0)
    return pl.pallas_call(
        _out_ln_kernel,
        name="out_ln",
        grid=(n // tm,),
        in_specs=[
            pl.BlockSpec((tm, D_MODEL), lambda i: (i, 0)),
            pl.BlockSpec((tm, GLA_WIDTH), lambda i: (i, 0)),
            pl.BlockSpec((tm, DSA_WIDTH), lambda i: (i, 0)),
            pl.BlockSpec((GLA_WIDTH, D_MODEL), const),
            pl.BlockSpec((DSA_WIDTH, D_MODEL), const),
            pl.BlockSpec((1, D_MODEL), const),
            pl.BlockSpec((1, D_MODEL), const),
        ],
        out_specs=pl.BlockSpec((tm, D_MODEL), lambda i: (i, 0)),
        out_shape=jax.ShapeDtypeStruct((n, D_MODEL), F32),
        compiler_params=pltpu.CompilerParams(
            dimension_semantics=("arbitrary",), vmem_limit_bytes=VMEM_LIMIT_BYTES),
    )(x2d, og, od, wg, wd, g.reshape(1, D_MODEL), b.reshape(1, D_MODEL))


def kernel(x, w_in, w_gla_a2, b_gla_a, gla_norm_g, w_out, ffn1_w_gu, ffn1_w_down,
           ffn2_w_gu, ffn2_w_down, ln1_g, ln1_b, ln2_g, ln2_b, ln3_g, ln3_b):
    bsz, t, d = x.shape
    n = bsz * t
    for l in range(DEPTH):
        x1 = _ffn_ln(x.reshape(n, d), ffn1_w_gu[l], ffn1_w_down[l], ln1_g[l], ln1_b[l])
        gq, gk, la, gv, gg, dq, dk, iq, ik2, gvT, dvT, iwT = _in_proj(
            x1.reshape(bsz, t, d), w_in[l], w_gla_a2[l], b_gla_a[l])
        o_gla = _gla(gq, gk, la, gv, gvT, gg, gla_norm_g[l])
        o_dsaT = _dsa(dq, iq, iwT, dk, dvT, ik2)
        o_dsa = jnp.swapaxes(o_dsaT, 1, 2)
        x2 = _out_ln(x1, o_gla.reshape(n, GLA_WIDTH), o_dsa.reshape(n, DSA_WIDTH),
                     w_out[l], ln2_g[l], ln2_b[l])
        x = _ffn_ln(x2, ffn2_w_gu[l], ffn2_w_down[l], ln3_g[l], ln3_b[l]).reshape(bsz, t, d)
    return x
```

```python
import functools

import jax
import jax.numpy as jnp
from jax import lax
from jax.experimental import pallas as pl
from jax.experimental.pallas import tpu as pltpu

F32 = jnp.float32
BF16 = jnp.bfloat16

D_MODEL = 1024
D_FF = 2816
CHUNK = 64
GLA_HEADS = 4
GLA_DK = 64
GLA_DV = 128
GLA_QK = GLA_HEADS * GLA_DK
GLA_WIDTH = GLA_HEADS * GLA_DV
GLA_LOWRANK = 16
GLA_TAU = 16.0
DSA_HEADS = 8
DSA_DH = 64
DSA_WIDTH = DSA_HEADS * DSA_DH
IDX_HEADS = 8
IDX_DIM = 64
TOPK_MAX = 256
LN_EPS = 1e-5
RMS_EPS = 1e-6
DEPTH = 1
DEEPNORM_ALPHA = (2.0 * DEPTH) ** 0.25

LANES = 128
SUBLANES = 8
VMEM_LIMIT_BYTES = 56 * 1024 * 1024

FFN_TM = 512
FFN_FC = 256
PROJ_TM = 512
GLA_TT = 512
GLA_C = 128
GLA_MAX_CHUNK_DECAY = 60.0
GLA_SEQ_ROWS = 8
DSA_TQ = 256
DSA_TK = 256
NEG_BIG = -1e30
INT_MIN = -(2 ** 31)
HALF_BITS = 16
HALF_MASK = (1 << HALF_BITS) - 1
I16_MIN = -(2 ** 15)
PACKED_SUBLANES = 16

_NT = (((1,), (1,)), ((), ()))


def _layer_norm(y, g, b):
    mu = jnp.mean(y, axis=-1, keepdims=True)
    yc = y - mu
    var = jnp.mean(yc * yc, axis=-1, keepdims=True)
    return yc * lax.rsqrt(var + LN_EPS) * g + b


def _silu(x):
    return x * jax.nn.sigmoid(x)


_TN = (((0,), (0,)), ((), ()))


def _ffn_ln_kernel(*refs, with_mix):
    if with_mix:
        (x_ref, og_ref, odT_ref, wog_ref, wod_ref, g0_ref, b0_ref,
         wg_ref, wu_ref, wd_ref, g_ref, b_ref, o_ref, xb_ref, acc_ref) = refs
        mix = (jnp.dot(og_ref[...], wog_ref[...], preferred_element_type=F32)
               + lax.dot_general(odT_ref[0], wod_ref[...], _TN, preferred_element_type=F32))
        x = _layer_norm(DEEPNORM_ALPHA * x_ref[...] + mix, g0_ref[...], b0_ref[...])
    else:
        x_ref, wg_ref, wu_ref, wd_ref, g_ref, b_ref, o_ref, xb_ref, acc_ref = refs
        x = x_ref[...]
    xb_ref[...] = x.astype(BF16)
    acc_ref[...] = jnp.zeros_like(acc_ref)

    def body(c, carry):
        xb = xb_ref[...]
        g = jnp.dot(xb, wg_ref[c], preferred_element_type=F32)
        u = jnp.dot(xb, wu_ref[c], preferred_element_type=F32)
        a = (_silu(g) * u).astype(BF16)
        acc_ref[...] += jnp.dot(a, wd_ref[c], preferred_element_type=F32)
        return carry

    lax.fori_loop(0, D_FF // FFN_FC, body, 0, unroll=True)
    y = DEEPNORM_ALPHA * x + 0.5 * acc_ref[...]
    o_ref[...] = _layer_norm(y, g_ref[...], b_ref[...])


def _ffn_ln(x2d, w_gu, w_down, g, b, mix=None):
    n = x2d.shape[0]
    tm = FFN_TM
    nc = D_FF // FFN_FC
    wg = w_gu[:, :D_FF].astype(BF16).reshape(D_MODEL, nc, FFN_FC).transpose(1, 0, 2)
    wu = w_gu[:, D_FF:].astype(BF16).reshape(D_MODEL, nc, FFN_FC).transpose(1, 0, 2)
    wd = w_down.astype(BF16).reshape(nc, FFN_FC, D_MODEL)
    full3 = lambda i: (0, 0, 0)
    const = lambda i: (0, 0)
    vec = pl.BlockSpec((1, D_MODEL), const)
    row = pl.BlockSpec((tm, D_MODEL), lambda i: (i, 0))
    operands, in_specs = [x2d], [row]
    if mix is not None:
        og, odT, w_out, g0, b0 = mix
        tpb = odT.shape[2] // tm
        operands += [og, odT, w_out[:GLA_WIDTH].astype(BF16), w_out[GLA_WIDTH:].astype(BF16),
                     g0.reshape(1, D_MODEL), b0.reshape(1, D_MODEL)]
        in_specs += [
            pl.BlockSpec((tm, GLA_WIDTH), lambda i: (i, 0)),
            pl.BlockSpec((1, DSA_WIDTH, tm), lambda i: (i // tpb, 0, i % tpb)),
            pl.BlockSpec((GLA_WIDTH, D_MODEL), const),
            pl.BlockSpec((DSA_WIDTH, D_MODEL), const),
            vec, vec,
        ]
    operands += [wg, wu, wd, g.reshape(1, D_MODEL), b.reshape(1, D_MODEL)]
    in_specs += [
        pl.BlockSpec((nc, D_MODEL, FFN_FC), full3, pipeline_mode=pl.Buffered(1)),
        pl.BlockSpec((nc, D_MODEL, FFN_FC), full3, pipeline_mode=pl.Buffered(1)),
        pl.BlockSpec((nc, FFN_FC, D_MODEL), full3, pipeline_mode=pl.Buffered(1)),
        vec, vec,
    ]
    return pl.pallas_call(
        functools.partial(_ffn_ln_kernel, with_mix=mix is not None),
        name="mix_ffn_ln" if mix is not None else "ffn_ln",
        grid=(n // tm,),
        in_specs=in_specs,
        out_specs=row,
        out_shape=jax.ShapeDtypeStruct((n, D_MODEL), F32),
        scratch_shapes=[
            pltpu.VMEM((tm, D_MODEL), BF16),
            pltpu.VMEM((tm, D_MODEL), F32),
        ],
        compiler_params=pltpu.CompilerParams(
            dimension_semantics=("arbitrary",), vmem_limit_bytes=VMEM_LIMIT_BYTES),
    )(*operands)


_N_GQ, _N_GK, _N_GV, _N_GG, _N_GA, _N_DQ, _N_DK, _N_IQ, _N_IK, _N_END = (
    0, 256, 512, 1024, 1536, 1664, 2176, 2688, 3200, 3328)
_T_GV, _T_DV, _T_IW, _T_END = 0, 512, 1024, 1040


def _proj_kernel(x_ref, wn_ref, wt_ref, wa2_ref, ba_ref,
                 gq_ref, gk_ref, la_ref, gv_ref, gg_ref, dq_ref, dk_ref, iq_ref, ik_ref,
                 gvT_ref, dvT_ref, iwT_ref):
    xb = x_ref[0].astype(BF16)

    def nproj(lo, hi):
        return jnp.dot(xb, wn_ref[:, lo:hi], preferred_element_type=F32)

    def tproj(lo, hi):
        return lax.dot_general(wt_ref[lo:hi, :], xb, _NT, preferred_element_type=F32)

    gq_ref[0] = nproj(_N_GQ, _N_GK)
    gk_ref[0] = nproj(_N_GK, _N_GV)
    gv_ref[0] = nproj(_N_GV, _N_GG).astype(BF16)
    gg_ref[0] = nproj(_N_GG, _N_GA)
    ga = nproj(_N_GA, _N_DQ)
    z = jnp.dot(ga.astype(BF16), wa2_ref[...], preferred_element_type=F32) + ba_ref[...]
    la_ref[0] = jax.nn.log_sigmoid(z) * (1.0 / GLA_TAU)
    dq_ref[0] = (nproj(_N_DQ, _N_DK) * (DSA_DH ** -0.5)).astype(BF16)
    dk_ref[0] = nproj(_N_DK, _N_IQ).astype(BF16)
    iq_ref[0] = (nproj(_N_IQ, _N_IK) * (IDX_DIM ** -0.5)).astype(BF16)
    ik_ref[0] = nproj(_N_IK, _N_END).astype(BF16)
    gvT_ref[0] = tproj(_T_GV, _T_DV).astype(BF16)
    dvT = tproj(_T_DV, _T_IW).astype(BF16)
    for s in range(PROJ_TM // DSA_TK):
        dvT_ref[0, s] = dvT[:, s * DSA_TK:(s + 1) * DSA_TK]
    iwT_ref[0] = tproj(_T_IW, _T_END)[:IDX_HEADS, :] * (IDX_HEADS ** -0.5)


def _in_proj(x1, w_in, w_a2, b_a):
    bsz, t, _ = x1.shape
    o = [0, 256, 512, 1024, 1536, 1552, 2064, 2576, 3088, 3600, 3664, 3672]
    col = lambda i: w_in[:, o[i]:o[i + 1]]
    pad = lambda w, n: jnp.pad(w, ((0, 0), (0, n - w.shape[1])))
    wn = jnp.concatenate(
        [col(0), col(1), col(2), col(3), pad(col(4), LANES), col(5), col(6), col(8), col(9), col(9)],
        axis=1).astype(BF16)
    wt = jnp.concatenate([col(2), col(7), pad(col(10), 16)], axis=1).T.astype(BF16)
    wa2 = jnp.pad(w_a2, ((0, LANES - GLA_LOWRANK), (0, 0))).astype(BF16)
    ba = b_a.reshape(1, GLA_QK)
    tm = PROJ_TM
    spt = tm // DSA_TK
    row = lambda w: pl.BlockSpec((1, tm, w), lambda b, i: (b, i, 0))
    const2 = lambda b, i: (0, 0)
    out_shape = (
        jax.ShapeDtypeStruct((bsz, t, GLA_QK), F32),
        jax.ShapeDtypeStruct((bsz, t, GLA_QK), F32),
        jax.ShapeDtypeStruct((bsz, t, GLA_QK), F32),
        jax.ShapeDtypeStruct((bsz, t, GLA_WIDTH), BF16),
        jax.ShapeDtypeStruct((bsz, t, GLA_WIDTH), F32),
        jax.ShapeDtypeStruct((bsz, t, DSA_WIDTH), BF16),
        jax.ShapeDtypeStruct((bsz, t, DSA_WIDTH), BF16),
        jax.ShapeDtypeStruct((bsz, t, IDX_HEADS * IDX_DIM), BF16),
        jax.ShapeDtypeStruct((bsz, t, 2 * IDX_DIM), BF16),
        jax.ShapeDtypeStruct((bsz, GLA_WIDTH, t), BF16),
        jax.ShapeDtypeStruct((bsz, t // DSA_TK, DSA_WIDTH, DSA_TK), BF16),
        jax.ShapeDtypeStruct((bsz, IDX_HEADS, t), F32),
    )
    out_specs = (
        row(GLA_QK), row(GLA_QK), row(GLA_QK), row(GLA_WIDTH), row(GLA_WIDTH),
        row(DSA_WIDTH), row(DSA_WIDTH), row(IDX_HEADS * IDX_DIM), row(2 * IDX_DIM),
        pl.BlockSpec((1, GLA_WIDTH, tm), lambda b, i: (b, 0, i)),
        pl.BlockSpec((1, spt, DSA_WIDTH, DSA_TK), lambda b, i: (b, i, 0, 0)),
        pl.BlockSpec((1, IDX_HEADS, tm), lambda b, i: (b, 0, i)),
    )
    return pl.pallas_call(
        _proj_kernel,
        name="in_proj",
        grid=(bsz, t // tm),
        in_specs=[
            pl.BlockSpec((1, tm, D_MODEL), lambda b, i: (b, i, 0)),
            pl.BlockSpec((D_MODEL, _N_END), const2, pipeline_mode=pl.Buffered(1)),
            pl.BlockSpec((_T_END, D_MODEL), const2, pipeline_mode=pl.Buffered(1)),
            pl.BlockSpec((LANES, GLA_QK), const2),
            pl.BlockSpec((1, GLA_QK), const2),
        ],
        out_specs=out_specs,
        out_shape=out_shape,
        compiler_params=pltpu.CompilerParams(
            dimension_semantics=("arbitrary", "arbitrary"), vmem_limit_bytes=VMEM_LIMIT_BYTES),
    )(x1, wn, wt, wa2, ba)


def _gla_kernel(q_ref, k_ref, la_ref, v_ref, vT_ref, gg_ref, g_ref, o_ref, s_ref, of_ref):
    c = GLA_C

    @pl.when(pl.program_id(1) == 0)
    def _():
        s_ref[...] = jnp.zeros_like(s_ref)

    lane = lax.broadcasted_iota(jnp.int32, (c, LANES), 1)
    halves = (lane < GLA_DK, lane >= GLA_DK)
    gnorm = g_ref[...]

    def finish(o, rows, hl):
        ms = jnp.mean(o * o, axis=-1, keepdims=True)
        on = o * lax.rsqrt(ms + RMS_EPS) * gnorm
        return on * _silu(gg_ref[0, rows, hl])

    worst = jnp.zeros((1, GLA_QK), F32)
    for ch in range(GLA_TT // c):
        worst = jnp.minimum(
            worst, jnp.sum(la_ref[0, ch * c:(ch + 1) * c, :], axis=0, keepdims=True))
    mild = jnp.min(worst) >= -GLA_MAX_CHUNK_DECAY

    @pl.when(mild)
    def _():
        ri = lax.broadcasted_iota(jnp.int32, (c, c), 0)
        ci = lax.broadcasted_iota(jnp.int32, (c, c), 1)
        causal = ci <= ri
        tri = jnp.where(causal, 1.0, 0.0).astype(BF16)
        for ch in range(GLA_TT // c):
            rows = slice(ch * c, (ch + 1) * c)
            la = la_ref[0, rows, :]
            la_hi = la.astype(BF16)
            la_lo = (la - la_hi.astype(F32)).astype(BF16)
            b = (jnp.dot(tri, la_hi, preferred_element_type=F32)
                 + jnp.dot(tri, la_lo, preferred_element_type=F32))
            b_last = b[c - 1:c, :]
            q = q_ref[0, rows, :] * (GLA_DK ** -0.5)
            k = k_ref[0, rows, :]
            qd = (q * jnp.exp(b)).astype(BF16)
            kd = (k * jnp.exp(-b)).astype(BF16)
            kdl = (k * jnp.exp(b_last - b)).astype(BF16)
            dec = jnp.exp(b_last)
            for h in range(GLA_HEADS):
                p, hh = divmod(h, 2)
                pl_ = slice(p * LANES, (p + 1) * LANES)
                qd_p = qd[:, pl_]
                kd_m = jnp.where(halves[hh], kd[:, pl_], 0.0).astype(BF16)
                kdl_m = jnp.where(halves[hh], kdl[:, pl_], 0.0).astype(BF16)
                a = lax.dot_general(qd_p, kd_m, _NT, preferred_element_type=F32)
                a = jnp.where(causal, a, 0.0).astype(BF16)
                hl = slice(h * GLA_DV, (h + 1) * GLA_DV)
                o = jnp.dot(a, v_ref[0, rows, hl], preferred_element_type=F32)
                st = s_ref[h]
                o = o + lax.dot_general(qd_p, st.astype(BF16), _NT, preferred_element_type=F32)
                s_ref[h] = st * dec[:, pl_] + jnp.dot(
                    vT_ref[0, hl, rows], kdl_m, preferred_element_type=F32)
                o_ref[0, rows, hl] = finish(o, rows, hl).astype(o_ref.dtype)

    @pl.when(jnp.logical_not(mild))
    def _():
        nb = GLA_SEQ_ROWS
        tpos = lax.broadcasted_iota(jnp.int32, (GLA_TT, LANES), 0)
        half_row = (halves[0][:1], halves[1][:1])

        def block(tb, carry):
            r0 = pl.multiple_of(tb * nb, nb)
            qb = q_ref[0, pl.ds(r0, nb), :] * (GLA_DK ** -0.5)
            kb = k_ref[0, pl.ds(r0, nb), :]
            ab = jnp.exp(la_ref[0, pl.ds(r0, nb), :])
            outs = [[] for _ in range(GLA_HEADS)]
            for r in range(nb):
                pick = jnp.where(tpos == r0 + r, 1.0, 0.0).astype(BF16)
                for h in range(GLA_HEADS):
                    p, hh = divmod(h, 2)
                    pl_ = slice(p * LANES, (p + 1) * LANES)
                    hl = slice(h * GLA_DV, (h + 1) * GLA_DV)
                    vcol = jnp.dot(vT_ref[0, hl, :], pick, preferred_element_type=F32)
                    k_m = jnp.where(half_row[hh], kb[r:r + 1, pl_], 0.0)
                    st = s_ref[h] * ab[r:r + 1, pl_] + vcol * k_m
                    s_ref[h] = st
                    q_rep = jnp.broadcast_to(qb[r:r + 1, pl_], (PACKED_SUBLANES, LANES)).astype(BF16)
                    o = lax.dot_general(q_rep, st.astype(BF16), _NT, preferred_element_type=F32)
                    outs[h].append(o[0:1, :])
            for h in range(GLA_HEADS):
                of_ref[pl.ds(r0, nb), h * GLA_DV:(h + 1) * GLA_DV] = jnp.concatenate(outs[h], axis=0)
            return carry

        lax.fori_loop(0, GLA_TT // nb, block, 0)
        for h in range(GLA_HEADS):
            hl = slice(h * GLA_DV, (h + 1) * GLA_DV)
            o_ref[0, :, hl] = finish(of_ref[:, hl], slice(None), hl).astype(o_ref.dtype)


def _gla(gq, gk, la, gv, gvT, gg, gnorm):
    bsz, t, _ = gq.shape
    tt = GLA_TT
    row = lambda w: pl.BlockSpec((1, tt, w), lambda b, i: (b, i, 0))
    return pl.pallas_call(
        _gla_kernel,
        name="gla",
        grid=(bsz, t // tt),
        in_specs=[
            row(GLA_QK), row(GLA_QK), row(GLA_QK), row(GLA_WIDTH),
            pl.BlockSpec((1, GLA_WIDTH, tt), lambda b, i: (b, 0, i)),
            row(GLA_WIDTH),
            pl.BlockSpec((1, GLA_DV), lambda b, i: (0, 0)),
        ],
        out_specs=row(GLA_WIDTH),
        out_shape=jax.ShapeDtypeStruct((bsz, t, GLA_WIDTH), BF16),
        scratch_shapes=[pltpu.VMEM((GLA_HEADS, GLA_DV, LANES), F32),
                        pltpu.VMEM((GLA_TT, GLA_WIDTH), F32)],
        compiler_params=pltpu.CompilerParams(
            dimension_semantics=("arbitrary", "arbitrary"), vmem_limit_bytes=VMEM_LIMIT_BYTES),
    )(gq, gk, la, gv, gvT, gg, gnorm.reshape(1, GLA_DV))


def _dsa_kernel(q_ref, qi_ref, wT_ref, k_ref, vT_ref, ki_ref, oT_ref,
                key_ref, hi_ref, lo_ref, bias_ref, qm_ref, qim_ref, s_ref, p_ref,
                m_ref, l_ref, al_ref, acc_ref, *, topk):
    tq, tk = DSA_TQ, DSA_TK
    j = pl.program_id(1)
    nt = j + 1
    q0 = j * tq

    lane = lax.broadcasted_iota(jnp.int32, (tq, LANES), 1)
    halves = (lane < DSA_DH, lane >= DSA_DH)
    qcol = q0 + lax.broadcasted_iota(jnp.int32, (1, tq), 1)
    lim = (qcol // CHUNK + 1) * CHUNK
    krow = lax.broadcasted_iota(jnp.int32, (tk, tq), 0)

    for h in range(DSA_HEADS):
        p, hh = divmod(h, 2)
        pl_ = slice(p * LANES, (p + 1) * LANES)
        qm_ref[h] = jnp.where(halves[hh], q_ref[0, :, pl_], 0.0).astype(BF16)
        qim_ref[h] = jnp.where(halves[hh], qi_ref[0, :, pl_], 0.0).astype(BF16)

    wT = wT_ref[0]

    def score_tile(t, carry):
        r0 = pl.multiple_of(t * tk, tk)
        ki2 = ki_ref[0, pl.ds(r0, tk), :]
        for h in range(IDX_HEADS):
            s_ref[h] = lax.dot_general(ki2, qim_ref[h], _NT, preferred_element_type=F32)
        acc = jnp.zeros((tk, tq), F32)
        for h in range(IDX_HEADS):
            acc = acc + wT[h:h + 1, :] * jnp.maximum(s_ref[h], 0.0)
        bits = pltpu.bitcast(acc, jnp.int32)
        sk = bits ^ ((bits >> 31) & 0x7FFFFFFF)
        sk = jnp.where(sk == -1, 0, sk)
        sk = jnp.where(krow + r0 < lim, sk, INT_MIN)
        key_ref[t] = sk
        hi_ref[t] = (sk >> HALF_BITS).astype(jnp.int16)
        lo_ref[t] = ((sk & HALF_MASK) + I16_MIN).astype(jnp.int16)
        return carry

    lax.fori_loop(0, nt, score_tile, 0)

    def count16(ref, cand):
        cand16 = cand.astype(jnp.int16)

        def tile(t, acc):
            hit = jnp.where(ref[t] >= cand16, jnp.int16(1), jnp.int16(0))
            for g in range(tk // PACKED_SUBLANES):
                acc = acc + hit[g * PACKED_SUBLANES:(g + 1) * PACKED_SUBLANES, :]
            return acc
        acc = lax.fori_loop(0, nt, tile, jnp.zeros((PACKED_SUBLANES, tq), jnp.int16))
        return jnp.sum(acc.astype(jnp.int32), axis=0, keepdims=True)

    def hi_bit(i, thi):
        cand = thi + lax.shift_left(jnp.int32(1), HALF_BITS - 1 - i)
        return jnp.where(count16(hi_ref, cand) >= topk, cand, thi)

    thi = lax.fori_loop(0, HALF_BITS, hi_bit, jnp.full((1, tq), I16_MIN, jnp.int32))
    n_above = jnp.where(thi < -I16_MIN - 1, count16(hi_ref, jnp.minimum(thi + 1, -I16_MIN - 1)), 0)
    need_lo = topk - n_above

    thi16 = thi.astype(jnp.int16)

    def mask_lo(t, carry):
        lo_ref[t] = jnp.where(hi_ref[t] == thi16, lo_ref[t], jnp.int16(I16_MIN))
        return carry

    lax.fori_loop(0, nt, mask_lo, 0)

    def lo_bit(i, state):
        tlo, cnt = state
        cand = tlo + lax.shift_left(jnp.int32(1), HALF_BITS - 1 - i)
        c = count16(lo_ref, cand)
        take = c >= need_lo
        return jnp.where(take, cand, tlo), jnp.where(take, c, cnt)

    tlo, cnt = lax.fori_loop(
        0, HALF_BITS, lo_bit,
        (jnp.full((1, tq), I16_MIN, jnp.int32), jnp.full((1, tq), -1, jnp.int32)))
    has_tie = jnp.max(jnp.where(jnp.logical_and(cnt != need_lo, lim > topk), 1, 0))
    thr = thi * (1 << HALF_BITS) + (tlo - I16_MIN)
    thr = jnp.where(lim > topk, jnp.maximum(thr, INT_MIN + 1), INT_MIN + 1)

    @pl.when(has_tie == 0)
    def _():
        def tile(t, carry):
            bias_ref[t] = jnp.where(key_ref[t] >= thr, 0.0, NEG_BIG)
            return carry
        lax.fori_loop(0, nt, tile, 0)

    @pl.when(has_tie > 0)
    def _():
        def gt_tile(t, acc):
            hit = jnp.where(key_ref[t] > thr, 1.0, 0.0)
            for g in range(tk // SUBLANES):
                acc = acc + hit[g * SUBLANES:(g + 1) * SUBLANES, :]
            return acc
        n_gt = jnp.sum(lax.fori_loop(0, nt, gt_tile, jnp.zeros((SUBLANES, tq), F32)),
                       axis=0, keepdims=True)
        need = topk - n_gt
        ri = lax.broadcasted_iota(jnp.int32, (tk, tk), 0)
        ci = lax.broadcasted_iota(jnp.int32, (tk, tk), 1)
        tri = jnp.where(ci <= ri, 1.0, 0.0).astype(BF16)

        def tile(t, before):
            sk = key_ref[t]
            eq = sk == thr
            rank = before + jnp.dot(tri, jnp.where(eq, 1.0, 0.0).astype(BF16),
                                    preferred_element_type=F32)
            tie_bias = jnp.where(rank <= need, 0.0, NEG_BIG)
            bias_ref[t] = jnp.where(sk > thr, 0.0, jnp.where(eq, tie_bias, NEG_BIG))
            return rank[tk - 1:tk, :]
        lax.fori_loop(0, nt, tile, jnp.zeros((1, tq), F32))

    m_ref[...] = jnp.full(m_ref.shape, NEG_BIG, F32)
    l_ref[...] = jnp.zeros(l_ref.shape, F32)
    acc_ref[...] = jnp.zeros(acc_ref.shape, F32)

    def att_tile(t, carry):
        r0 = pl.multiple_of(t * tk, tk)
        for h in range(DSA_HEADS):
            p = h // 2
            kt = k_ref[0, pl.ds(r0, tk), p * LANES:(p + 1) * LANES]
            s_ref[h] = lax.dot_general(kt, qm_ref[h], _NT, preferred_element_type=F32)
        bias = bias_ref[t]
        for h in range(DSA_HEADS):
            s = s_ref[h] + bias
            m_old = m_ref[h:h + 1, :]
            m_new = jnp.maximum(m_old, jnp.max(s, axis=0, keepdims=True))
            alpha = jnp.exp(m_old - m_new)
            pr = jnp.exp(s - m_new)
            p_ref[h] = pr.astype(BF16)
            l_ref[h:h + 1, :] = alpha * l_ref[h:h + 1, :] + jnp.sum(pr, axis=0, keepdims=True)
            m_ref[h:h + 1, :] = m_new
            al_ref[h:h + 1, :] = alpha
        for h in range(DSA_HEADS):
            vt = vT_ref[0, t, h * DSA_DH:(h + 1) * DSA_DH, :]
            acc_ref[h] = al_ref[h:h + 1, :] * acc_ref[h] + jnp.dot(
                vt, p_ref[h], preferred_element_type=F32)
        return carry

    lax.fori_loop(0, nt, att_tile, 0)
    for h in range(DSA_HEADS):
        oT_ref[0, h * DSA_DH:(h + 1) * DSA_DH, :] = (
            acc_ref[h] / l_ref[h:h + 1, :]).astype(oT_ref.dtype)


def _dsa(dq, iq, iwT, dk, dvT, ik2):
    bsz, t, _ = dq.shape
    tq, tk = DSA_TQ, DSA_TK
    assert tq == tk and t % tq == 0
    nt = t // tk
    assert nt * (tk // PACKED_SUBLANES) < -I16_MIN
    topk = min(TOPK_MAX, t // 4)
    row = lambda w: pl.BlockSpec((1, tq, w), lambda b, i: (b, i, 0))
    return pl.pallas_call(
        functools.partial(_dsa_kernel, topk=topk),
        name="dsa",
        grid=(bsz, t // tq),
        in_specs=[
            row(DSA_WIDTH), row(IDX_HEADS * IDX_DIM),
            pl.BlockSpec((1, IDX_HEADS, tq), lambda b, i: (b, 0, i)),
            pl.BlockSpec((1, t, DSA_WIDTH), lambda b, i: (b, 0, 0)),
            pl.BlockSpec((1, nt, DSA_WIDTH, tk), lambda b, i: (b, 0, 0, 0)),
            pl.BlockSpec((1, t, 2 * IDX_DIM), lambda b, i: (b, 0, 0)),
        ],
        out_specs=pl.BlockSpec((1, DSA_WIDTH, tq), lambda b, i: (b, 0, i)),
        out_shape=jax.ShapeDtypeStruct((bsz, DSA_WIDTH, t), BF16),
        scratch_shapes=[
            pltpu.VMEM((nt, tk, tq), jnp.int32),
            pltpu.VMEM((nt, tk, tq), jnp.int16),
            pltpu.VMEM((nt, tk, tq), jnp.int16),
            pltpu.VMEM((nt, tk, tq), F32),
            pltpu.VMEM((DSA_HEADS, tq, LANES), BF16),
            pltpu.VMEM((IDX_HEADS, tq, LANES), BF16),
            pltpu.VMEM((DSA_HEADS, tk, tq), F32),
            pltpu.VMEM((DSA_HEADS, tk, tq), BF16),
            pltpu.VMEM((DSA_HEADS, tq), F32),
            pltpu.VMEM((DSA_HEADS, tq), F32),
            pltpu.VMEM((DSA_HEADS, tq), F32),
            pltpu.VMEM((DSA_HEADS, DSA_DH, tq), F32),
        ],
        compiler_params=pltpu.CompilerParams(
            dimension_semantics=("arbitrary", "arbitrary"), vmem_limit_bytes=VMEM_LIMIT_BYTES),
    )(dq, iq, iwT, dk, dvT, ik2)


def kernel(x, w_in, w_gla_a2, b_gla_a, gla_norm_g, w_out, ffn1_w_gu, ffn1_w_down,
           ffn2_w_gu, ffn2_w_down, ln1_g, ln1_b, ln2_g, ln2_b, ln3_g, ln3_b):
    bsz, t, d = x.shape
    n = bsz * t
    for l in range(DEPTH):
        x1 = _ffn_ln(x.reshape(n, d), ffn1_w_gu[l], ffn1_w_down[l], ln1_g[l], ln1_b[l])
        gq, gk, la, gv, gg, dq, dk, iq, ik2, gvT, dvT, iwT = _in_proj(
            x1.reshape(bsz, t, d), w_in[l], w_gla_a2[l], b_gla_a[l])
        o_gla = _gla(gq, gk, la, gv, gvT, gg, gla_norm_g[l])
        o_dsaT = _dsa(dq, iq, iwT, dk, dvT, ik2)
        x = _ffn_ln(x1, ffn2_w_gu[l], ffn2_w_down[l], ln3_g[l], ln3_b[l],
                    mix=(o_gla.reshape(n, GLA_WIDTH), o_dsaT, w_out[l], ln2_g[l], ln2_b[l]),
                    ).reshape(bsz, t, d)
    return x
```

```python
import functools

import jax
import jax.numpy as jnp
from jax import lax
from jax.experimental import pallas as pl
from jax.experimental.pallas import tpu as pltpu

F32 = jnp.float32
BF16 = jnp.bfloat16

D_MODEL = 1024
D_FF = 2816
CHUNK = 64
GLA_HEADS = 4
GLA_DK = 64
GLA_DV = 128
GLA_QK = GLA_HEADS * GLA_DK
GLA_WIDTH = GLA_HEADS * GLA_DV
GLA_LOWRANK = 16
GLA_TAU = 16.0
DSA_HEADS = 8
DSA_DH = 64
DSA_WIDTH = DSA_HEADS * DSA_DH
IDX_HEADS = 8
IDX_DIM = 64
TOPK_MAX = 256
LN_EPS = 1e-5
RMS_EPS = 1e-6
DEPTH = 1
DEEPNORM_ALPHA = (2.0 * DEPTH) ** 0.25
LOG2_E = 1.4426950408889634

LANES = 128
SUBLANES = 8
VMEM_LIMIT_BYTES = 56 * 1024 * 1024

FFN_TM = 512
FFN_FC = 256
PROJ_TM = 512
GLA_TT = 512
GLA_C = 128
GLA_MAX_CHUNK_DECAY = 60.0
GLA_SEQ_ROWS = 8
DSA_TQ = 256
DSA_TK = 256
NEG_BIG = -1e30
INT_MIN = -(2 ** 31)
HALF_BITS = 16
HALF_MASK = (1 << HALF_BITS) - 1
I16_MIN = -(2 ** 15)
PACKED_SUBLANES = 16

_NT = (((1,), (1,)), ((), ()))


def _layer_norm(y, g, b):
    mu = jnp.mean(y, axis=-1, keepdims=True)
    yc = y - mu
    var = jnp.mean(yc * yc, axis=-1, keepdims=True)
    return yc * lax.rsqrt(var + LN_EPS) * g + b


def _silu(x):
    return x * jax.nn.sigmoid(x)


_TN = (((0,), (0,)), ((), ()))


def _ffn_ln_kernel(*refs, with_mix):
    if with_mix:
        (x_ref, og_ref, odT_ref, wog_ref, wod_ref, g0_ref, b0_ref,
         wgu_ref, wd_ref, g_ref, b_ref, o_ref, xb_ref, acc_ref) = refs
        mix = (jnp.dot(og_ref[...], wog_ref[...], preferred_element_type=F32)
               + lax.dot_general(odT_ref[0], wod_ref[...], _TN, preferred_element_type=F32))
        x = _layer_norm(DEEPNORM_ALPHA * x_ref[...] + mix, g0_ref[...], b0_ref[...])
    else:
        x_ref, wgu_ref, wd_ref, g_ref, b_ref, o_ref, xb_ref, acc_ref = refs
        x = x_ref[...]
    xb_ref[...] = x.astype(BF16)
    acc_ref[...] = jnp.zeros_like(acc_ref)

    for c in range(D_FF // FFN_FC):
        cols = slice(c * FFN_FC, (c + 1) * FFN_FC)
        xb = xb_ref[...]
        g = jnp.dot(xb, wgu_ref[:, cols], preferred_element_type=F32)
        u = jnp.dot(xb, wgu_ref[:, D_FF + c * FFN_FC:D_FF + (c + 1) * FFN_FC],
                    preferred_element_type=F32)
        a = (_silu(g) * u).astype(BF16)
        acc_ref[...] += jnp.dot(a, wd_ref[cols, :], preferred_element_type=F32)
    y = DEEPNORM_ALPHA * x + 0.5 * acc_ref[...]
    o_ref[...] = _layer_norm(y, g_ref[...], b_ref[...])


def _ffn_ln(x2d, w_gu, w_down, g, b, mix=None):
    n = x2d.shape[0]
    tm = FFN_TM
    const = lambda i: (0, 0)
    vec = pl.BlockSpec((1, D_MODEL), const)
    row = pl.BlockSpec((tm, D_MODEL), lambda i: (i, 0))
    operands, in_specs = [x2d], [row]
    if mix is not None:
        og, odT, w_out, g0, b0 = mix
        tpb = odT.shape[2] // tm
        operands += [og, odT, w_out[:GLA_WIDTH].astype(BF16), w_out[GLA_WIDTH:].astype(BF16),
                     g0.reshape(1, D_MODEL), b0.reshape(1, D_MODEL)]
        in_specs += [
            pl.BlockSpec((tm, GLA_WIDTH), lambda i: (i, 0)),
            pl.BlockSpec((1, DSA_WIDTH, tm), lambda i: (i // tpb, 0, i % tpb)),
            pl.BlockSpec((GLA_WIDTH, D_MODEL), const),
            pl.BlockSpec((DSA_WIDTH, D_MODEL), const),
            vec, vec,
        ]
    operands += [w_gu.astype(BF16), w_down.astype(BF16),
                 g.reshape(1, D_MODEL), b.reshape(1, D_MODEL)]
    in_specs += [
        pl.BlockSpec((D_MODEL, 2 * D_FF), const, pipeline_mode=pl.Buffered(1)),
        pl.BlockSpec((D_FF, D_MODEL), const, pipeline_mode=pl.Buffered(1)),
        vec, vec,
    ]
    return pl.pallas_call(
        functools.partial(_ffn_ln_kernel, with_mix=mix is not None),
        name="mix_ffn_ln" if mix is not None else "ffn_ln",
        grid=(n // tm,),
        in_specs=in_specs,
        out_specs=row,
        out_shape=jax.ShapeDtypeStruct((n, D_MODEL), F32),
        scratch_shapes=[
            pltpu.VMEM((tm, D_MODEL), BF16),
            pltpu.VMEM((tm, D_MODEL), F32),
        ],
        compiler_params=pltpu.CompilerParams(
            dimension_semantics=("arbitrary",), vmem_limit_bytes=VMEM_LIMIT_BYTES),
    )(*operands)


_N_GQ, _N_GK, _N_GV, _N_GG, _N_GA, _N_DQ, _N_DK, _N_IQ, _N_IK, _N_END = (
    0, 256, 512, 1024, 1536, 1664, 2176, 2688, 3200, 3328)
_T_GV, _T_DV, _T_IW, _T_END = 0, 512, 1024, 1040


def _proj_kernel(x_ref, wn_ref, wt_ref, wa2_ref, ba_ref,
                 gq_ref, gk_ref, la_ref, gv_ref, gg_ref, dq_ref, dk_ref, iq_ref, ik_ref,
                 gvT_ref, dvT_ref, iwT_ref):
    xb = x_ref[0].astype(BF16)

    def nproj(lo, hi):
        return jnp.dot(xb, wn_ref[:, lo:hi], preferred_element_type=F32)

    def tproj(lo, hi):
        return lax.dot_general(wt_ref[lo:hi, :], xb, _NT, preferred_element_type=F32)

    gq_ref[0] = nproj(_N_GQ, _N_GK)
    gk_ref[0] = nproj(_N_GK, _N_GV)
    gv_ref[0] = nproj(_N_GV, _N_GG).astype(BF16)
    gg_ref[0] = nproj(_N_GG, _N_GA)
    ga = nproj(_N_GA, _N_DQ)
    z = jnp.dot(ga.astype(BF16), wa2_ref[...], preferred_element_type=F32) + ba_ref[...]
    la_ref[0] = jax.nn.log_sigmoid(z) * (1.0 / GLA_TAU)
    dq_ref[0] = (nproj(_N_DQ, _N_DK) * (DSA_DH ** -0.5 * LOG2_E)).astype(BF16)
    dk_ref[0] = nproj(_N_DK, _N_IQ).astype(BF16)
    iq_ref[0] = (nproj(_N_IQ, _N_IK) * (IDX_DIM ** -0.5)).astype(BF16)
    ik_ref[0] = nproj(_N_IK, _N_END).astype(BF16)
    gvT_ref[0] = tproj(_T_GV, _T_DV).astype(BF16)
    dvT = tproj(_T_DV, _T_IW).astype(BF16)
    for s in range(PROJ_TM // DSA_TK):
        dvT_ref[0, s] = dvT[:, s * DSA_TK:(s + 1) * DSA_TK]
    iwT_ref[0] = tproj(_T_IW, _T_END)[:IDX_HEADS, :] * (IDX_HEADS ** -0.5)


def _in_proj(x1, w_in, w_a2, b_a):
    bsz, t, _ = x1.shape
    o = [0, 256, 512, 1024, 1536, 1552, 2064, 2576, 3088, 3600, 3664, 3672]
    col = lambda i: w_in[:, o[i]:o[i + 1]]
    pad = lambda w, n: jnp.pad(w, ((0, 0), (0, n - w.shape[1])))
    wn = jnp.concatenate(
        [col(0), col(1), col(2), col(3), pad(col(4), LANES), col(5), col(6), col(8), col(9), col(9)],
        axis=1).astype(BF16)
    wt = jnp.concatenate([col(2), col(7), pad(col(10), 16)], axis=1).T.astype(BF16)
    wa2 = jnp.pad(w_a2, ((0, LANES - GLA_LOWRANK), (0, 0))).astype(BF16)
    ba = b_a.reshape(1, GLA_QK)
    tm = PROJ_TM
    spt = tm // DSA_TK
    row = lambda w: pl.BlockSpec((1, tm, w), lambda b, i: (b, i, 0))
    const2 = lambda b, i: (0, 0)
    out_shape = (
        jax.ShapeDtypeStruct((bsz, t, GLA_QK), F32),
        jax.ShapeDtypeStruct((bsz, t, GLA_QK), F32),
        jax.ShapeDtypeStruct((bsz, t, GLA_QK), F32),
        jax.ShapeDtypeStruct((bsz, t, GLA_WIDTH), BF16),
        jax.ShapeDtypeStruct((bsz, t, GLA_WIDTH), F32),
        jax.ShapeDtypeStruct((bsz, t, DSA_WIDTH), BF16),
        jax.ShapeDtypeStruct((bsz, t, DSA_WIDTH), BF16),
        jax.ShapeDtypeStruct((bsz, t, IDX_HEADS * IDX_DIM), BF16),
        jax.ShapeDtypeStruct((bsz, t, 2 * IDX_DIM), BF16),
        jax.ShapeDtypeStruct((bsz, GLA_WIDTH, t), BF16),
        jax.ShapeDtypeStruct((bsz, t // DSA_TK, DSA_WIDTH, DSA_TK), BF16),
        jax.ShapeDtypeStruct((bsz, IDX_HEADS, t), F32),
    )
    out_specs = (
        row(GLA_QK), row(GLA_QK), row(GLA_QK), row(GLA_WIDTH), row(GLA_WIDTH),
        row(DSA_WIDTH), row(DSA_WIDTH), row(IDX_HEADS * IDX_DIM), row(2 * IDX_DIM),
        pl.BlockSpec((1, GLA_WIDTH, tm), lambda b, i: (b, 0, i)),
        pl.BlockSpec((1, spt, DSA_WIDTH, DSA_TK), lambda b, i: (b, i, 0, 0)),
        pl.BlockSpec((1, IDX_HEADS, tm), lambda b, i: (b, 0, i)),
    )
    return pl.pallas_call(
        _proj_kernel,
        name="in_proj",
        grid=(bsz, t // tm),
        in_specs=[
            pl.BlockSpec((1, tm, D_MODEL), lambda b, i: (b, i, 0)),
            pl.BlockSpec((D_MODEL, _N_END), const2, pipeline_mode=pl.Buffered(1)),
            pl.BlockSpec((_T_END, D_MODEL), const2, pipeline_mode=pl.Buffered(1)),
            pl.BlockSpec((LANES, GLA_QK), const2),
            pl.BlockSpec((1, GLA_QK), const2),
        ],
        out_specs=out_specs,
        out_shape=out_shape,
        compiler_params=pltpu.CompilerParams(
            dimension_semantics=("arbitrary", "arbitrary"), vmem_limit_bytes=VMEM_LIMIT_BYTES),
    )(x1, wn, wt, wa2, ba)


def _gla_kernel(q_ref, k_ref, la_ref, v_ref, vT_ref, gg_ref, g_ref, o_ref, s_ref, of_ref):
    c = GLA_C

    @pl.when(pl.program_id(1) == 0)
    def _():
        s_ref[...] = jnp.zeros_like(s_ref)

    lane = lax.broadcasted_iota(jnp.int32, (c, LANES), 1)
    halves = (lane < GLA_DK, lane >= GLA_DK)
    gnorm = g_ref[...]

    def finish(o, rows, hl):
        ms = jnp.mean(o * o, axis=-1, keepdims=True)
        on = o * lax.rsqrt(ms + RMS_EPS) * gnorm
        return on * _silu(gg_ref[0, rows, hl])

    worst = jnp.zeros((1, GLA_QK), F32)
    for ch in range(GLA_TT // c):
        worst = jnp.minimum(
            worst, jnp.sum(la_ref[0, ch * c:(ch + 1) * c, :], axis=0, keepdims=True))
    mild = jnp.min(worst) >= -GLA_MAX_CHUNK_DECAY

    @pl.when(mild)
    def _():
        ri = lax.broadcasted_iota(jnp.int32, (c, c), 0)
        ci = lax.broadcasted_iota(jnp.int32, (c, c), 1)
        causal = ci <= ri
        tri = jnp.where(causal, 1.0, 0.0).astype(BF16)
        for ch in range(GLA_TT // c):
            rows = slice(ch * c, (ch + 1) * c)
            la = la_ref[0, rows, :]
            la_hi = la.astype(BF16)
            la_lo = (la - la_hi.astype(F32)).astype(BF16)
            b = (jnp.dot(tri, la_hi, preferred_element_type=F32)
                 + jnp.dot(tri, la_lo, preferred_element_type=F32))
            b_last = b[c - 1:c, :]
            q = q_ref[0, rows, :] * (GLA_DK ** -0.5)
            k = k_ref[0, rows, :]
            qd = (q * jnp.exp(b)).astype(BF16)
            kd = (k * jnp.exp(-b)).astype(BF16)
            kdl = (k * jnp.exp(b_last - b)).astype(BF16)
            dec = jnp.exp(b_last)
            for h in range(GLA_HEADS):
                p, hh = divmod(h, 2)
                pl_ = slice(p * LANES, (p + 1) * LANES)
                qd_p = qd[:, pl_]
                kd_m = jnp.where(halves[hh], kd[:, pl_], 0.0).astype(BF16)
                kdl_m = jnp.where(halves[hh], kdl[:, pl_], 0.0).astype(BF16)
                a = lax.dot_general(qd_p, kd_m, _NT, preferred_element_type=F32)
                a = jnp.where(causal, a, 0.0).astype(BF16)
                hl = slice(h * GLA_DV, (h + 1) * GLA_DV)
                o = jnp.dot(a, v_ref[0, rows, hl], preferred_element_type=F32)
                st = s_ref[h]
                o = o + lax.dot_general(qd_p, st.astype(BF16), _NT, preferred_element_type=F32)
                s_ref[h] = st * dec[:, pl_] + jnp.dot(
                    vT_ref[0, hl, rows], kdl_m, preferred_element_type=F32)
                o_ref[0, rows, hl] = finish(o, rows, hl).astype(o_ref.dtype)

    @pl.when(jnp.logical_not(mild))
    def _():
        nb = GLA_SEQ_ROWS
        tpos = lax.broadcasted_iota(jnp.int32, (GLA_TT, LANES), 0)
        half_row = (halves[0][:1], halves[1][:1])

        def block(tb, carry):
            r0 = pl.multiple_of(tb * nb, nb)
            qb = q_ref[0, pl.ds(r0, nb), :] * (GLA_DK ** -0.5)
            kb = k_ref[0, pl.ds(r0, nb), :]
            ab = jnp.exp(la_ref[0, pl.ds(r0, nb), :])
            outs = [[] for _ in range(GLA_HEADS)]
            for r in range(nb):
                pick = jnp.where(tpos == r0 + r, 1.0, 0.0).astype(BF16)
                for h in range(GLA_HEADS):
                    p, hh = divmod(h, 2)
                    pl_ = slice(p * LANES, (p + 1) * LANES)
                    hl = slice(h * GLA_DV, (h + 1) * GLA_DV)
                    vcol = jnp.dot(vT_ref[0, hl, :], pick, preferred_element_type=F32)
                    k_m = jnp.where(half_row[hh], kb[r:r + 1, pl_], 0.0)
                    st = s_ref[h] * ab[r:r + 1, pl_] + vcol * k_m
                    s_ref[h] = st
                    q_rep = jnp.broadcast_to(qb[r:r + 1, pl_], (PACKED_SUBLANES, LANES)).astype(BF16)
                    o = lax.dot_general(q_rep, st.astype(BF16), _NT, preferred_element_type=F32)
                    outs[h].append(o[0:1, :])
            for h in range(GLA_HEADS):
                of_ref[pl.ds(r0, nb), h * GLA_DV:(h + 1) * GLA_DV] = jnp.concatenate(outs[h], axis=0)
            return carry

        lax.fori_loop(0, GLA_TT // nb, block, 0)
        for h in range(GLA_HEADS):
            hl = slice(h * GLA_DV, (h + 1) * GLA_DV)
            o_ref[0, :, hl] = finish(of_ref[:, hl], slice(None), hl).astype(o_ref.dtype)


def _gla(gq, gk, la, gv, gvT, gg, gnorm):
    bsz, t, _ = gq.shape
    tt = GLA_TT
    row = lambda w: pl.BlockSpec((1, tt, w), lambda b, i: (b, i, 0))
    return pl.pallas_call(
        _gla_kernel,
        name="gla",
        grid=(bsz, t // tt),
        in_specs=[
            row(GLA_QK), row(GLA_QK), row(GLA_QK), row(GLA_WIDTH),
            pl.BlockSpec((1, GLA_WIDTH, tt), lambda b, i: (b, 0, i)),
            row(GLA_WIDTH),
            pl.BlockSpec((1, GLA_DV), lambda b, i: (0, 0)),
        ],
        out_specs=row(GLA_WIDTH),
        out_shape=jax.ShapeDtypeStruct((bsz, t, GLA_WIDTH), BF16),
        scratch_shapes=[pltpu.VMEM((GLA_HEADS, GLA_DV, LANES), F32),
                        pltpu.VMEM((GLA_TT, GLA_WIDTH), F32)],
        compiler_params=pltpu.CompilerParams(
            dimension_semantics=("arbitrary", "arbitrary"), vmem_limit_bytes=VMEM_LIMIT_BYTES),
    )(gq, gk, la, gv, gvT, gg, gnorm.reshape(1, GLA_DV))


def _tile_pair_loop(nt, tile_fn, init):
    def body(i, carry):
        for u in range(2):
            carry = tile_fn(2 * i + u, carry)
        return carry
    return lax.fori_loop(0, lax.shift_right_logical(nt + 1, 1), body, init)


def _dsa_kernel(q_ref, qi_ref, wT_ref, k_ref, vT_ref, ki_ref, oT_ref,
                key_ref, hi_ref, lo_ref, bias_ref, qm_ref, qim_ref, s_ref, p_ref,
                m_ref, l_ref, al_ref, acc_ref, *, topk):
    tq, tk = DSA_TQ, DSA_TK
    j = pl.program_id(1)
    nt = j + 1
    q0 = j * tq

    lane = lax.broadcasted_iota(jnp.int32, (tq, LANES), 1)
    halves = (lane < DSA_DH, lane >= DSA_DH)
    qcol = q0 + lax.broadcasted_iota(jnp.int32, (1, tq), 1)
    lim = (qcol // CHUNK + 1) * CHUNK
    krow = lax.broadcasted_iota(jnp.int32, (tk, tq), 0)

    for h in range(DSA_HEADS):
        p, hh = divmod(h, 2)
        pl_ = slice(p * LANES, (p + 1) * LANES)
        qm_ref[h] = jnp.where(halves[hh], q_ref[0, :, pl_], 0.0).astype(BF16)
        qim_ref[h] = jnp.where(halves[hh], qi_ref[0, :, pl_], 0.0).astype(BF16)

    wT = wT_ref[0]

    def score_tile(t, carry):
        r0 = pl.multiple_of(t * tk, tk)
        ki2 = ki_ref[0, pl.ds(r0, tk), :]
        acc = jnp.zeros((tk, tq), F32)
        for h in range(IDX_HEADS):
            s = lax.dot_general(ki2, qim_ref[h], _NT, preferred_element_type=F32)
            acc = acc + wT[h:h + 1, :] * jnp.maximum(s, 0.0)
        bits = pltpu.bitcast(acc, jnp.int32)
        sk = bits ^ ((bits >> 31) & 0x7FFFFFFF)
        sk = jnp.where(sk == -1, 0, sk)
        sk = jnp.where(krow + r0 < lim, sk, INT_MIN)
        key_ref[t] = sk
        hi_ref[t] = (sk >> HALF_BITS).astype(jnp.int16)
        lo_ref[t] = ((sk & HALF_MASK) + I16_MIN).astype(jnp.int16)
        return carry

    _tile_pair_loop(nt, score_tile, 0)

    def count16(ref, cand):
        cand16 = cand.astype(jnp.int16)

        def tile(t, acc):
            hit = jnp.where(ref[t] >= cand16, jnp.int16(1), jnp.int16(0))
            parts = [hit[g * PACKED_SUBLANES:(g + 1) * PACKED_SUBLANES, :]
                     for g in range(tk // PACKED_SUBLANES)]
            while len(parts) > 1:
                parts = [a + b for a, b in zip(parts[::2], parts[1::2])]
            return acc + parts[0]
        acc = _tile_pair_loop(nt, tile, jnp.zeros((PACKED_SUBLANES, tq), jnp.int16))
        return jnp.sum(acc.astype(jnp.int32), axis=0, keepdims=True)

    def hi_bit(i, state):
        thi, cnt = state
        cand = thi + lax.shift_left(jnp.int32(1), HALF_BITS - 1 - i)
        c = count16(hi_ref, cand)
        take = c >= topk
        return jnp.where(take, cand, thi), jnp.where(take, c, cnt)

    thi, cnt_hi = lax.fori_loop(
        0, HALF_BITS, hi_bit,
        (jnp.full((1, tq), I16_MIN, jnp.int32), jnp.full((1, tq), -1, jnp.int32)))
    n_above = jnp.where(thi < -I16_MIN - 1, count16(hi_ref, jnp.minimum(thi + 1, -I16_MIN - 1)), 0)
    need_lo = topk - n_above

    thi16 = thi.astype(jnp.int16)

    def mask_lo(t, carry):
        lo_ref[t] = jnp.where(hi_ref[t] == thi16, lo_ref[t], jnp.int16(I16_MIN))
        return carry

    _tile_pair_loop(nt, mask_lo, 0)

    def lo_bit(i, state):
        tlo, cnt = state
        cand = tlo + lax.shift_left(jnp.int32(1), HALF_BITS - 1 - i)
        c = count16(lo_ref, cand)
        take = c >= need_lo
        return jnp.where(take, cand, tlo), jnp.where(take, c, cnt)

    tlo, cnt = lax.fori_loop(
        0, HALF_BITS, lo_bit,
        (jnp.full((1, tq), I16_MIN, jnp.int32), jnp.full((1, tq), -1, jnp.int32)))
    has_tie = jnp.max(jnp.where(jnp.logical_and(cnt != need_lo, lim > topk), 1, 0))
    thr = thi * (1 << HALF_BITS) + (tlo - I16_MIN)
    thr = jnp.where(lim > topk, jnp.maximum(thr, INT_MIN + 1), INT_MIN + 1)

    @pl.when(has_tie == 0)
    def _():
        def tile(t, carry):
            bias_ref[t] = jnp.where(key_ref[t] >= thr, 0.0, NEG_BIG)
            return carry
        _tile_pair_loop(nt, tile, 0)

    @pl.when(has_tie > 0)
    def _():
        in_bucket = jnp.where(cnt < 0, cnt_hi - n_above, cnt)
        excess = (n_above + in_bucket - topk).astype(F32)
        ri = lax.broadcasted_iota(jnp.int32, (tk, tk), 0)
        ci = lax.broadcasted_iota(jnp.int32, (tk, tk), 1)
        later = jnp.where(ci > ri, 1.0, 0.0).astype(BF16)

        def tile(t, after):
            sk = key_ref[t]
            eq = sk == thr
            ones = jnp.where(eq, 1.0, 0.0)
            behind = after + jnp.dot(later, ones.astype(BF16), preferred_element_type=F32)
            tie_bias = jnp.where(behind >= excess, 0.0, NEG_BIG)
            bias_ref[t] = jnp.where(sk > thr, 0.0, jnp.where(eq, tie_bias, NEG_BIG))
            return behind[0:1, :] + ones[0:1, :]

        n_pairs = lax.shift_right_logical(nt + 1, 1)

        def pair(i, after):
            p = n_pairs - 1 - i
            return tile(2 * p, tile(2 * p + 1, after))
        lax.fori_loop(0, n_pairs, pair, jnp.zeros((1, tq), F32))

    m_ref[...] = jnp.full(m_ref.shape, NEG_BIG, F32)
    l_ref[...] = jnp.zeros(l_ref.shape, F32)
    acc_ref[...] = jnp.zeros(acc_ref.shape, F32)

    def att_tile(t, carry):
        r0 = pl.multiple_of(t * tk, tk)
        bias = bias_ref[t]
        for h in range(DSA_HEADS):
            p = h // 2
            kt = k_ref[0, pl.ds(r0, tk), p * LANES:(p + 1) * LANES]
            s = lax.dot_general(kt, qm_ref[h], _NT, preferred_element_type=F32) + bias
            s_ref[h] = s
            al_ref[h:h + 1, :] = jnp.max(s, axis=0, keepdims=True)
        for h in range(DSA_HEADS):
            m_old = m_ref[h:h + 1, :]
            m_new = jnp.maximum(m_old, al_ref[h:h + 1, :])
            alpha = jnp.exp2(m_old - m_new)
            pr = jnp.exp2(s_ref[h] - m_new)
            p_ref[h] = pr.astype(BF16)
            l_ref[h:h + 1, :] = alpha * l_ref[h:h + 1, :] + jnp.sum(pr, axis=0, keepdims=True)
            m_ref[h:h + 1, :] = m_new
            al_ref[h:h + 1, :] = alpha
        for h in range(DSA_HEADS):
            vt = vT_ref[0, t, h * DSA_DH:(h + 1) * DSA_DH, :]
            acc_ref[h] = al_ref[h:h + 1, :] * acc_ref[h] + jnp.dot(
                vt, p_ref[h], preferred_element_type=F32)
        return carry

    lax.fori_loop(0, nt, att_tile, 0)
    for h in range(DSA_HEADS):
        oT_ref[0, h * DSA_DH:(h + 1) * DSA_DH, :] = (
            acc_ref[h] / l_ref[h:h + 1, :]).astype(oT_ref.dtype)


def _dsa(dq, iq, iwT, dk, dvT, ik2):
    bsz, t, _ = dq.shape
    tq, tk = DSA_TQ, DSA_TK
    assert tq == tk and t % tq == 0
    nt = t // tk
    assert (nt + 1) * (tk // PACKED_SUBLANES) < -I16_MIN
    assert nt % 2 == 0
    topk = min(TOPK_MAX, t // 4)
    row = lambda w: pl.BlockSpec((1, tq, w), lambda b, i: (b, i, 0))
    return pl.pallas_call(
        functools.partial(_dsa_kernel, topk=topk),
        name="dsa",
        grid=(bsz, t // tq),
        in_specs=[
            row(DSA_WIDTH), row(IDX_HEADS * IDX_DIM),
            pl.BlockSpec((1, IDX_HEADS, tq), lambda b, i: (b, 0, i)),
            pl.BlockSpec((1, t, DSA_WIDTH), lambda b, i: (b, 0, 0)),
            pl.BlockSpec((1, nt, DSA_WIDTH, tk), lambda b, i: (b, 0, 0, 0)),
            pl.BlockSpec((1, t, 2 * IDX_DIM), lambda b, i: (b, 0, 0)),
        ],
        out_specs=pl.BlockSpec((1, DSA_WIDTH, tq), lambda b, i: (b, 0, i)),
        out_shape=jax.ShapeDtypeStruct((bsz, DSA_WIDTH, t), BF16),
        scratch_shapes=[
            pltpu.VMEM((nt, tk, tq), jnp.int32),
            pltpu.VMEM((nt, tk, tq), jnp.int16),
            pltpu.VMEM((nt, tk, tq), jnp.int16),
            pltpu.VMEM((nt, tk, tq), F32),
            pltpu.VMEM((DSA_HEADS, tq, LANES), BF16),
            pltpu.VMEM((IDX_HEADS, tq, LANES), BF16),
            pltpu.VMEM((DSA_HEADS, tk, tq), F32),
            pltpu.VMEM((DSA_HEADS, tk, tq), BF16),
            pltpu.VMEM((DSA_HEADS, tq), F32),
            pltpu.VMEM((DSA_HEADS, tq), F32),
            pltpu.VMEM((DSA_HEADS, tq), F32),
            pltpu.VMEM((DSA_HEADS, DSA_DH, tq), F32),
        ],
        compiler_params=pltpu.CompilerParams(
            dimension_semantics=("arbitrary", "arbitrary"), vmem_limit_bytes=VMEM_LIMIT_BYTES),
    )(dq, iq, iwT, dk, dvT, ik2)


def kernel(x, w_in, w_gla_a2, b_gla_a, gla_norm_g, w_out, ffn1_w_gu, ffn1_w_down,
           ffn2_w_gu, ffn2_w_down, ln1_g, ln1_b, ln2_g, ln2_b, ln3_g, ln3_b):
    bsz, t, d = x.shape
    n = bsz * t
    for l in range(DEPTH):
        x1 = _ffn_ln(x.reshape(n, d), ffn1_w_gu[l], ffn1_w_down[l], ln1_g[l], ln1_b[l])
        gq, gk, la, gv, gg, dq, dk, iq, ik2, gvT, dvT, iwT = _in_proj(
            x1.reshape(bsz, t, d), w_in[l], w_gla_a2[l], b_gla_a[l])
        o_gla = _gla(gq, gk, la, gv, gvT, gg, gla_norm_g[l])
        o_dsaT = _dsa(dq, iq, iwT, dk, dvT, ik2)
        x = _ffn_ln(x1, ffn2_w_gu[l], ffn2_w_down[l], ln3_g[l], ln3_b[l],
                    mix=(o_gla.reshape(n, GLA_WIDTH), o_dsaT, w_out[l], ln2_g[l], ln2_b[l]),
                    ).reshape(bsz, t, d)
    return x
```

```python
import functools

import jax
import jax.numpy as jnp
from jax import lax
from jax.experimental import pallas as pl
from jax.experimental.pallas import tpu as pltpu

F32 = jnp.float32
BF16 = jnp.bfloat16

D_MODEL = 1024
D_FF = 2816
CHUNK = 64
GLA_HEADS = 4
GLA_DK = 64
GLA_DV = 128
GLA_QK = GLA_HEADS * GLA_DK
GLA_WIDTH = GLA_HEADS * GLA_DV
GLA_LOWRANK = 16
GLA_TAU = 16.0
DSA_HEADS = 8
DSA_DH = 64
DSA_WIDTH = DSA_HEADS * DSA_DH
IDX_HEADS = 8
IDX_DIM = 64
TOPK_MAX = 256
LN_EPS = 1e-5
RMS_EPS = 1e-6
DEPTH = 1
DEEPNORM_ALPHA = (2.0 * DEPTH) ** 0.25
LOG2_E = 1.4426950408889634

LANES = 128
SUBLANES = 8
VMEM_LIMIT_BYTES = 56 * 1024 * 1024

FFN_TM = 512
FFN_FC = 256
PROJ_TM = 512
GLA_TT = 512
GLA_C = 128
GLA_MAX_CHUNK_DECAY = 60.0
GLA_SEQ_ROWS = 8
DSA_TQ = 256
DSA_TK = 256
NEG_BIG = -1e30
INT_MIN = -(2 ** 31)
HALF_BITS = 16
HALF_MASK = (1 << HALF_BITS) - 1
I16_MIN = -(2 ** 15)
PACKED_SUBLANES = 16

_NT = (((1,), (1,)), ((), ()))


def _layer_norm(y, g, b):
    mu = jnp.mean(y, axis=-1, keepdims=True)
    yc = y - mu
    var = jnp.mean(yc * yc, axis=-1, keepdims=True)
    return yc * lax.rsqrt(var + LN_EPS) * g + b


def _silu(x):
    return x * jax.nn.sigmoid(x)


_TN = (((0,), (0,)), ((), ()))


def _ffn_ln_kernel(*refs, with_mix):
    if with_mix:
        (x_ref, og_ref, odT_ref, wog_ref, wod_ref, g0_ref, b0_ref,
         wgu_ref, wd_ref, g_ref, b_ref, o_ref, xb_ref, acc_ref) = refs
        mix = (jnp.dot(og_ref[...], wog_ref[...], preferred_element_type=F32)
               + lax.dot_general(odT_ref[0], wod_ref[...], _TN, preferred_element_type=F32))
        x = _layer_norm(DEEPNORM_ALPHA * x_ref[...] + mix, g0_ref[...], b0_ref[...])
    else:
        x_ref, wgu_ref, wd_ref, g_ref, b_ref, o_ref, xb_ref, acc_ref = refs
        x = x_ref[...]
    xb_ref[...] = x.astype(BF16)
    acc_ref[...] = jnp.zeros_like(acc_ref)

    for c in range(D_FF // FFN_FC):
        cols = slice(c * FFN_FC, (c + 1) * FFN_FC)
        xb = xb_ref[...]
        g = jnp.dot(xb, wgu_ref[:, cols], preferred_element_type=F32)
        u = jnp.dot(xb, wgu_ref[:, D_FF + c * FFN_FC:D_FF + (c + 1) * FFN_FC],
                    preferred_element_type=F32)
        a = (_silu(g) * u).astype(BF16)
        acc_ref[...] += jnp.dot(a, wd_ref[cols, :], preferred_element_type=F32)
    y = DEEPNORM_ALPHA * x + 0.5 * acc_ref[...]
    o_ref[...] = _layer_norm(y, g_ref[...], b_ref[...])


def _ffn_ln(x2d, w_gu, w_down, g, b, mix=None):
    n = x2d.shape[0]
    tm = FFN_TM
    const = lambda i: (0, 0)
    vec = pl.BlockSpec((1, D_MODEL), const)
    row = pl.BlockSpec((tm, D_MODEL), lambda i: (i, 0))
    operands, in_specs = [x2d], [row]
    if mix is not None:
        og, odT, w_out, g0, b0 = mix
        tpb = odT.shape[2] // tm
        operands += [og, odT, w_out[:GLA_WIDTH].astype(BF16), w_out[GLA_WIDTH:].astype(BF16),
                     g0.reshape(1, D_MODEL), b0.reshape(1, D_MODEL)]
        in_specs += [
            pl.BlockSpec((tm, GLA_WIDTH), lambda i: (i, 0)),
            pl.BlockSpec((1, DSA_WIDTH, tm), lambda i: (i // tpb, 0, i % tpb)),
            pl.BlockSpec((GLA_WIDTH, D_MODEL), const),
            pl.BlockSpec((DSA_WIDTH, D_MODEL), const),
            vec, vec,
        ]
    operands += [w_gu.astype(BF16), w_down.astype(BF16),
                 g.reshape(1, D_MODEL), b.reshape(1, D_MODEL)]
    in_specs += [
        pl.BlockSpec((D_MODEL, 2 * D_FF), const, pipeline_mode=pl.Buffered(1)),
        pl.BlockSpec((D_FF, D_MODEL), const, pipeline_mode=pl.Buffered(1)),
        vec, vec,
    ]
    return pl.pallas_call(
        functools.partial(_ffn_ln_kernel, with_mix=mix is not None),
        name="mix_ffn_ln" if mix is not None else "ffn_ln",
        grid=(n // tm,),
        in_specs=in_specs,
        out_specs=row,
        out_shape=jax.ShapeDtypeStruct((n, D_MODEL), F32),
        scratch_shapes=[
            pltpu.VMEM((tm, D_MODEL), BF16),
            pltpu.VMEM((tm, D_MODEL), F32),
        ],
        compiler_params=pltpu.CompilerParams(
            dimension_semantics=("arbitrary",), vmem_limit_bytes=VMEM_LIMIT_BYTES),
    )(*operands)


_N_GQ, _N_GK, _N_GV, _N_GG, _N_GA, _N_DQ, _N_DK, _N_IQ, _N_IK, _N_END = (
    0, 256, 512, 1024, 1536, 1664, 2176, 2688, 3200, 3328)
_T_GV, _T_DV, _T_IW, _T_END = 0, 512, 1024, 1040


def _proj_kernel(x_ref, wn_ref, wt_ref, wa2_ref, ba_ref,
                 gq_ref, gk_ref, la_ref, gv_ref, gg_ref, dq_ref, dk_ref, iq_ref, ik_ref,
                 gvT_ref, dvT_ref, iwT_ref):
    xb = x_ref[0].astype(BF16)

    def nproj(lo, hi):
        return jnp.dot(xb, wn_ref[:, lo:hi], preferred_element_type=F32)

    def tproj(lo, hi):
        return lax.dot_general(wt_ref[lo:hi, :], xb, _NT, preferred_element_type=F32)

    gq_ref[0] = nproj(_N_GQ, _N_GK)
    gk_ref[0] = nproj(_N_GK, _N_GV)
    gv_ref[0] = nproj(_N_GV, _N_GG).astype(BF16)
    gg_ref[0] = nproj(_N_GG, _N_GA)
    ga = nproj(_N_GA, _N_DQ)
    z = jnp.dot(ga.astype(BF16), wa2_ref[...], preferred_element_type=F32) + ba_ref[...]
    la_ref[0] = jax.nn.log_sigmoid(z) * (1.0 / GLA_TAU)
    dq_ref[0] = (nproj(_N_DQ, _N_DK) * (DSA_DH ** -0.5 * LOG2_E)).astype(BF16)
    dk_ref[0] = nproj(_N_DK, _N_IQ).astype(BF16)
    iq_ref[0] = (nproj(_N_IQ, _N_IK) * (IDX_DIM ** -0.5)).astype(BF16)
    ik_ref[0] = nproj(_N_IK, _N_END).astype(BF16)
    gvT_ref[0] = tproj(_T_GV, _T_DV).astype(BF16)
    dvT = tproj(_T_DV, _T_IW).astype(BF16)
    for s in range(PROJ_TM // DSA_TK):
        dvT_ref[0, s] = dvT[:, s * DSA_TK:(s + 1) * DSA_TK]
    iwT_ref[0] = tproj(_T_IW, _T_END)[:IDX_HEADS, :] * (IDX_HEADS ** -0.5)


def _in_proj(x1, w_in, w_a2, b_a):
    bsz, t, _ = x1.shape
    o = [0, 256, 512, 1024, 1536, 1552, 2064, 2576, 3088, 3600, 3664, 3672]
    col = lambda i: w_in[:, o[i]:o[i + 1]]
    pad = lambda w, n: jnp.pad(w, ((0, 0), (0, n - w.shape[1])))
    wn = jnp.concatenate(
        [col(0), col(1), col(2), col(3), pad(col(4), LANES), col(5), col(6), col(8), col(9), col(9)],
        axis=1).astype(BF16)
    wt = jnp.concatenate([col(2), col(7), pad(col(10), 16)], axis=1).T.astype(BF16)
    wa2 = jnp.pad(w_a2, ((0, LANES - GLA_LOWRANK), (0, 0))).astype(BF16)
    ba = b_a.reshape(1, GLA_QK)
    tm = PROJ_TM
    spt = tm // DSA_TK
    row = lambda w: pl.BlockSpec((1, tm, w), lambda b, i: (b, i, 0))
    const2 = lambda b, i: (0, 0)
    out_shape = (
        jax.ShapeDtypeStruct((bsz, t, GLA_QK), F32),
        jax.ShapeDtypeStruct((bsz, t, GLA_QK), F32),
        jax.ShapeDtypeStruct((bsz, t, GLA_QK), F32),
        jax.ShapeDtypeStruct((bsz, t, GLA_WIDTH), BF16),
        jax.ShapeDtypeStruct((bsz, t, GLA_WIDTH), F32),
        jax.ShapeDtypeStruct((bsz, t, DSA_WIDTH), BF16),
        jax.ShapeDtypeStruct((bsz, t, DSA_WIDTH), BF16),
        jax.ShapeDtypeStruct((bsz, t, IDX_HEADS * IDX_DIM), BF16),
        jax.ShapeDtypeStruct((bsz, t, 2 * IDX_DIM), BF16),
        jax.ShapeDtypeStruct((bsz, GLA_WIDTH, t), BF16),
        jax.ShapeDtypeStruct((bsz, t // DSA_TK, DSA_WIDTH, DSA_TK), BF16),
        jax.ShapeDtypeStruct((bsz, IDX_HEADS, t), F32),
    )
    out_specs = (
        row(GLA_QK), row(GLA_QK), row(GLA_QK), row(GLA_WIDTH), row(GLA_WIDTH),
        row(DSA_WIDTH), row(DSA_WIDTH), row(IDX_HEADS * IDX_DIM), row(2 * IDX_DIM),
        pl.BlockSpec((1, GLA_WIDTH, tm), lambda b, i: (b, 0, i)),
        pl.BlockSpec((1, spt, DSA_WIDTH, DSA_TK), lambda b, i: (b, i, 0, 0)),
        pl.BlockSpec((1, IDX_HEADS, tm), lambda b, i: (b, 0, i)),
    )
    return pl.pallas_call(
        _proj_kernel,
        name="in_proj",
        grid=(bsz, t // tm),
        in_specs=[
            pl.BlockSpec((1, tm, D_MODEL), lambda b, i: (b, i, 0)),
            pl.BlockSpec((D_MODEL, _N_END), const2, pipeline_mode=pl.Buffered(1)),
            pl.BlockSpec((_T_END, D_MODEL), const2, pipeline_mode=pl.Buffered(1)),
            pl.BlockSpec((LANES, GLA_QK), const2),
            pl.BlockSpec((1, GLA_QK), const2),
        ],
        out_specs=out_specs,
        out_shape=out_shape,
        compiler_params=pltpu.CompilerParams(
            dimension_semantics=("arbitrary", "arbitrary"), vmem_limit_bytes=VMEM_LIMIT_BYTES),
    )(x1, wn, wt, wa2, ba)


def _gla_kernel(q_ref, k_ref, la_ref, v_ref, vT_ref, gg_ref, g_ref, o_ref, s_ref, of_ref):
    c = GLA_C

    @pl.when(pl.program_id(1) == 0)
    def _():
        s_ref[...] = jnp.zeros_like(s_ref)

    lane = lax.broadcasted_iota(jnp.int32, (c, LANES), 1)
    halves = (lane < GLA_DK, lane >= GLA_DK)
    gnorm = g_ref[...]

    def finish(o, rows, hl):
        ms = jnp.mean(o * o, axis=-1, keepdims=True)
        on = o * lax.rsqrt(ms + RMS_EPS) * gnorm
        return on * _silu(gg_ref[0, rows, hl])

    worst = jnp.zeros((1, GLA_QK), F32)
    for ch in range(GLA_TT // c):
        worst = jnp.minimum(
            worst, jnp.sum(la_ref[0, ch * c:(ch + 1) * c, :], axis=0, keepdims=True))
    mild = jnp.min(worst) >= -GLA_MAX_CHUNK_DECAY

    @pl.when(mild)
    def _():
        ri = lax.broadcasted_iota(jnp.int32, (c, c), 0)
        ci = lax.broadcasted_iota(jnp.int32, (c, c), 1)
        causal = ci <= ri
        tri = jnp.where(causal, 1.0, 0.0).astype(BF16)
        for ch in range(GLA_TT // c):
            rows = slice(ch * c, (ch + 1) * c)
            la = la_ref[0, rows, :]
            la_hi = la.astype(BF16)
            la_lo = (la - la_hi.astype(F32)).astype(BF16)
            b = (jnp.dot(tri, la_hi, preferred_element_type=F32)
                 + jnp.dot(tri, la_lo, preferred_element_type=F32))
            b_last = b[c - 1:c, :]
            q = q_ref[0, rows, :] * (GLA_DK ** -0.5)
            k = k_ref[0, rows, :]
            qd = (q * jnp.exp(b)).astype(BF16)
            kd = (k * jnp.exp(-b)).astype(BF16)
            kdl = (k * jnp.exp(b_last - b)).astype(BF16)
            dec = jnp.exp(b_last)
            for h in range(GLA_HEADS):
                p, hh = divmod(h, 2)
                pl_ = slice(p * LANES, (p + 1) * LANES)
                qd_p = qd[:, pl_]
                kd_m = jnp.where(halves[hh], kd[:, pl_], 0.0).astype(BF16)
                kdl_m = jnp.where(halves[hh], kdl[:, pl_], 0.0).astype(BF16)
                a = lax.dot_general(qd_p, kd_m, _NT, preferred_element_type=F32)
                a = jnp.where(causal, a, 0.0).astype(BF16)
                hl = slice(h * GLA_DV, (h + 1) * GLA_DV)
                o = jnp.dot(a, v_ref[0, rows, hl], preferred_element_type=F32)
                st = s_ref[h]
                o = o + lax.dot_general(qd_p, st.astype(BF16), _NT, preferred_element_type=F32)
                s_ref[h] = st * dec[:, pl_] + jnp.dot(
                    vT_ref[0, hl, rows], kdl_m, preferred_element_type=F32)
                o_ref[0, rows, hl] = finish(o, rows, hl).astype(o_ref.dtype)

    @pl.when(jnp.logical_not(mild))
    def _():
        nb = GLA_SEQ_ROWS
        tpos = lax.broadcasted_iota(jnp.int32, (GLA_TT, LANES), 0)
        half_row = (halves[0][:1], halves[1][:1])

        def block(tb, carry):
            r0 = pl.multiple_of(tb * nb, nb)
            qb = q_ref[0, pl.ds(r0, nb), :] * (GLA_DK ** -0.5)
            kb = k_ref[0, pl.ds(r0, nb), :]
            ab = jnp.exp(la_ref[0, pl.ds(r0, nb), :])
            outs = [[] for _ in range(GLA_HEADS)]
            for r in range(nb):
                pick = jnp.where(tpos == r0 + r, 1.0, 0.0).astype(BF16)
                for h in range(GLA_HEADS):
                    p, hh = divmod(h, 2)
                    pl_ = slice(p * LANES, (p + 1) * LANES)
                    hl = slice(h * GLA_DV, (h + 1) * GLA_DV)
                    vcol = jnp.dot(vT_ref[0, hl, :], pick, preferred_element_type=F32)
                    k_m = jnp.where(half_row[hh], kb[r:r + 1, pl_], 0.0)
                    st = s_ref[h] * ab[r:r + 1, pl_] + vcol * k_m
                    s_ref[h] = st
                    q_rep = jnp.broadcast_to(qb[r:r + 1, pl_], (PACKED_SUBLANES, LANES)).astype(BF16)
                    o = lax.dot_general(q_rep, st.astype(BF16), _NT, preferred_element_type=F32)
                    outs[h].append(o[0:1, :])
            for h in range(GLA_HEADS):
                of_ref[pl.ds(r0, nb), h * GLA_DV:(h + 1) * GLA_DV] = jnp.concatenate(outs[h], axis=0)
            return carry

        lax.fori_loop(0, GLA_TT // nb, block, 0)
        for h in range(GLA_HEADS):
            hl = slice(h * GLA_DV, (h + 1) * GLA_DV)
            o_ref[0, :, hl] = finish(of_ref[:, hl], slice(None), hl).astype(o_ref.dtype)


def _gla(gq, gk, la, gv, gvT, gg, gnorm):
    bsz, t, _ = gq.shape
    tt = GLA_TT
    row = lambda w: pl.BlockSpec((1, tt, w), lambda b, i: (b, i, 0))
    return pl.pallas_call(
        _gla_kernel,
        name="gla",
        grid=(bsz, t // tt),
        in_specs=[
            row(GLA_QK), row(GLA_QK), row(GLA_QK), row(GLA_WIDTH),
            pl.BlockSpec((1, GLA_WIDTH, tt), lambda b, i: (b, 0, i)),
            row(GLA_WIDTH),
            pl.BlockSpec((1, GLA_DV), lambda b, i: (0, 0)),
        ],
        out_specs=row(GLA_WIDTH),
        out_shape=jax.ShapeDtypeStruct((bsz, t, GLA_WIDTH), BF16),
        scratch_shapes=[pltpu.VMEM((GLA_HEADS, GLA_DV, LANES), F32),
                        pltpu.VMEM((GLA_TT, GLA_WIDTH), F32)],
        compiler_params=pltpu.CompilerParams(
            dimension_semantics=("arbitrary", "arbitrary"), vmem_limit_bytes=VMEM_LIMIT_BYTES),
    )(gq, gk, la, gv, gvT, gg, gnorm.reshape(1, GLA_DV))


def _tile_pair_loop(nt, tile_fn, init):
    def body(i, carry):
        for u in range(2):
            carry = tile_fn(2 * i + u, carry)
        return carry
    return lax.fori_loop(0, lax.shift_right_logical(nt + 1, 1), body, init)


def _dsa_kernel(q_ref, qi_ref, wT_ref, k_ref, vT_ref, ki_ref, oT_ref,
                key_ref, hi_ref, lo_ref, bias_ref, qm_ref, qim_ref, s_ref, p_ref,
                m_ref, al_ref, acc_ref, *, topk):
    tq, tk = DSA_TQ, DSA_TK
    j = pl.program_id(1)
    nt = j + 1
    q0 = j * tq

    lane = lax.broadcasted_iota(jnp.int32, (tq, LANES), 1)
    halves = (lane < DSA_DH, lane >= DSA_DH)
    qcol = q0 + lax.broadcasted_iota(jnp.int32, (1, tq), 1)
    lim = (qcol // CHUNK + 1) * CHUNK
    krow = lax.broadcasted_iota(jnp.int32, (tk, tq), 0)

    for h in range(DSA_HEADS):
        p, hh = divmod(h, 2)
        pl_ = slice(p * LANES, (p + 1) * LANES)
        qm_ref[h] = jnp.where(halves[hh], q_ref[0, :, pl_], 0.0).astype(BF16)
        qim_ref[h] = jnp.where(halves[hh], qi_ref[0, :, pl_], 0.0).astype(BF16)

    wT = wT_ref[0]

    def score_tile(t, carry):
        r0 = pl.multiple_of(t * tk, tk)
        ki2 = ki_ref[0, pl.ds(r0, tk), :]
        acc = jnp.zeros((tk, tq), F32)
        for h in range(IDX_HEADS):
            s = lax.dot_general(ki2, qim_ref[h], _NT, preferred_element_type=F32)
            acc = acc + wT[h:h + 1, :] * jnp.maximum(s, 0.0)
        bits = pltpu.bitcast(acc, jnp.int32)
        sk = bits ^ ((bits >> 31) & 0x7FFFFFFF)
        sk = jnp.where(sk == -1, 0, sk)
        sk = jnp.where(krow + r0 < lim, sk, INT_MIN)
        key_ref[t] = sk
        hi_ref[t] = (sk >> HALF_BITS).astype(jnp.int16)
        lo_ref[t] = ((sk & HALF_MASK) + I16_MIN).astype(jnp.int16)
        return carry

    _tile_pair_loop(nt, score_tile, 0)

    def count16(ref, cand):
        cand16 = cand.astype(jnp.int16)

        def tile(t, acc):
            hit = jnp.where(ref[t] >= cand16, jnp.int16(1), jnp.int16(0))
            parts = [hit[g * PACKED_SUBLANES:(g + 1) * PACKED_SUBLANES, :]
                     for g in range(tk // PACKED_SUBLANES)]
            while len(parts) > 1:
                parts = [a + b for a, b in zip(parts[::2], parts[1::2])]
            return acc + parts[0]
        acc = _tile_pair_loop(nt, tile, jnp.zeros((PACKED_SUBLANES, tq), jnp.int16))
        return jnp.sum(acc.astype(jnp.int32), axis=0, keepdims=True)

    def hi_bit(i, state):
        thi, cnt = state
        cand = thi + lax.shift_left(jnp.int32(1), HALF_BITS - 1 - i)
        c = count16(hi_ref, cand)
        take = c >= topk
        return jnp.where(take, cand, thi), jnp.where(take, c, cnt)

    thi, cnt_hi = lax.fori_loop(
        0, HALF_BITS, hi_bit,
        (jnp.full((1, tq), I16_MIN, jnp.int32), jnp.full((1, tq), -1, jnp.int32)))
    n_above = jnp.where(thi < -I16_MIN - 1, count16(hi_ref, jnp.minimum(thi + 1, -I16_MIN - 1)), 0)
    need_lo = topk - n_above

    thi16 = thi.astype(jnp.int16)

    def mask_lo(t, carry):
        lo_ref[t] = jnp.where(hi_ref[t] == thi16, lo_ref[t], jnp.int16(I16_MIN))
        return carry

    _tile_pair_loop(nt, mask_lo, 0)

    def lo_bit(i, state):
        tlo, cnt = state
        cand = tlo + lax.shift_left(jnp.int32(1), HALF_BITS - 1 - i)
        c = count16(lo_ref, cand)
        take = c >= need_lo
        return jnp.where(take, cand, tlo), jnp.where(take, c, cnt)

    tlo, cnt = lax.fori_loop(
        0, HALF_BITS, lo_bit,
        (jnp.full((1, tq), I16_MIN, jnp.int32), jnp.full((1, tq), -1, jnp.int32)))
    has_tie = jnp.max(jnp.where(jnp.logical_and(cnt != need_lo, lim > topk), 1, 0))
    thr = thi * (1 << HALF_BITS) + (tlo - I16_MIN)
    thr = jnp.where(lim > topk, jnp.maximum(thr, INT_MIN + 1), INT_MIN + 1)

    @pl.when(has_tie == 0)
    def _():
        def tile(t, carry):
            bias_ref[t] = jnp.where(key_ref[t] >= thr, 0.0, NEG_BIG)
            return carry
        _tile_pair_loop(nt, tile, 0)

    @pl.when(has_tie > 0)
    def _():
        in_bucket = jnp.where(cnt < 0, cnt_hi - n_above, cnt)
        excess = (n_above + in_bucket - topk).astype(F32)
        ri = lax.broadcasted_iota(jnp.int32, (tk, tk), 0)
        ci = lax.broadcasted_iota(jnp.int32, (tk, tk), 1)
        later = jnp.where(ci > ri, 1.0, 0.0).astype(BF16)

        def tile(t, after):
            sk = key_ref[t]
            eq = sk == thr
            ones = jnp.where(eq, 1.0, 0.0)
            behind = after + jnp.dot(later, ones.astype(BF16), preferred_element_type=F32)
            tie_bias = jnp.where(behind >= excess, 0.0, NEG_BIG)
            bias_ref[t] = jnp.where(sk > thr, 0.0, jnp.where(eq, tie_bias, NEG_BIG))
            return behind[0:1, :] + ones[0:1, :]

        n_pairs = lax.shift_right_logical(nt + 1, 1)

        def pair(i, after):
            p = n_pairs - 1 - i
            return tile(2 * p, tile(2 * p + 1, after))
        lax.fori_loop(0, n_pairs, pair, jnp.zeros((1, tq), F32))

    m_ref[...] = jnp.full(m_ref.shape, NEG_BIG, F32)
    acc_ref[...] = jnp.zeros(acc_ref.shape, F32)

    def att_tile(t, carry):
        r0 = pl.multiple_of(t * tk, tk)
        bias = bias_ref[t]
        for h in range(DSA_HEADS):
            p = h // 2
            kt = k_ref[0, pl.ds(r0, tk), p * LANES:(p + 1) * LANES]
            s = lax.dot_general(kt, qm_ref[h], _NT, preferred_element_type=F32) + bias
            s_ref[h] = s
            al_ref[h:h + 1, :] = jnp.max(s, axis=0, keepdims=True)
        for h in range(DSA_HEADS):
            m_old = m_ref[h:h + 1, :]
            m_new = jnp.maximum(m_old, al_ref[h:h + 1, :])
            p_ref[h] = jnp.exp2(s_ref[h] - m_new).astype(BF16)
            m_ref[h:h + 1, :] = m_new
            al_ref[h:h + 1, :] = jnp.exp2(m_old - m_new)
        ones = jnp.ones((PACKED_SUBLANES, tk), BF16)
        for h in range(DSA_HEADS):
            vt = jnp.concatenate([vT_ref[0, t, h * DSA_DH:(h + 1) * DSA_DH, :], ones], axis=0)
            acc_ref[h] = al_ref[h:h + 1, :] * acc_ref[h] + jnp.dot(
                vt, p_ref[h], preferred_element_type=F32)
        return carry

    lax.fori_loop(0, nt, att_tile, 0)
    for h in range(DSA_HEADS):
        oT_ref[0, h * DSA_DH:(h + 1) * DSA_DH, :] = (
            acc_ref[h, :DSA_DH, :] / acc_ref[h, DSA_DH:DSA_DH + 1, :]).astype(oT_ref.dtype)


def _dsa(dq, iq, iwT, dk, dvT, ik2):
    bsz, t, _ = dq.shape
    tq, tk = DSA_TQ, DSA_TK
    assert tq == tk and t % tq == 0
    nt = t // tk
    assert (nt + 1) * (tk // PACKED_SUBLANES) < -I16_MIN
    assert nt % 2 == 0
    topk = min(TOPK_MAX, t // 4)
    row = lambda w: pl.BlockSpec((1, tq, w), lambda b, i: (b, i, 0))
    return pl.pallas_call(
        functools.partial(_dsa_kernel, topk=topk),
        name="dsa",
        grid=(bsz, t // tq),
        in_specs=[
            row(DSA_WIDTH), row(IDX_HEADS * IDX_DIM),
            pl.BlockSpec((1, IDX_HEADS, tq), lambda b, i: (b, 0, i)),
            pl.BlockSpec((1, t, DSA_WIDTH), lambda b, i: (b, 0, 0)),
            pl.BlockSpec((1, nt, DSA_WIDTH, tk), lambda b, i: (b, 0, 0, 0)),
            pl.BlockSpec((1, t, 2 * IDX_DIM), lambda b, i: (b, 0, 0)),
        ],
        out_specs=pl.BlockSpec((1, DSA_WIDTH, tq), lambda b, i: (b, 0, i)),
        out_shape=jax.ShapeDtypeStruct((bsz, DSA_WIDTH, t), BF16),
        scratch_shapes=[
            pltpu.VMEM((nt, tk, tq), jnp.int32),
            pltpu.VMEM((nt, tk, tq), jnp.int16),
            pltpu.VMEM((nt, tk, tq), jnp.int16),
            pltpu.VMEM((nt, tk, tq), F32),
            pltpu.VMEM((DSA_HEADS, tq, LANES), BF16),
            pltpu.VMEM((IDX_HEADS, tq, LANES), BF16),
            pltpu.VMEM((DSA_HEADS, tk, tq), F32),
            pltpu.VMEM((DSA_HEADS, tk, tq), BF16),
            pltpu.VMEM((DSA_HEADS, tq), F32),
            pltpu.VMEM((DSA_HEADS, tq), F32),
            pltpu.VMEM((DSA_HEADS, DSA_DH + PACKED_SUBLANES, tq), F32),
        ],
        compiler_params=pltpu.CompilerParams(
            dimension_semantics=("arbitrary", "arbitrary"), vmem_limit_bytes=VMEM_LIMIT_BYTES),
    )(dq, iq, iwT, dk, dvT, ik2)


def kernel(x, w_in, w_gla_a2, b_gla_a, gla_norm_g, w_out, ffn1_w_gu, ffn1_w_down,
           ffn2_w_gu, ffn2_w_down, ln1_g, ln1_b, ln2_g, ln2_b, ln3_g, ln3_b):
    bsz, t, d = x.shape
    n = bsz * t
    for l in range(DEPTH):
        x1 = _ffn_ln(x.reshape(n, d), ffn1_w_gu[l], ffn1_w_down[l], ln1_g[l], ln1_b[l])
        gq, gk, la, gv, gg, dq, dk, iq, ik2, gvT, dvT, iwT = _in_proj(
            x1.reshape(bsz, t, d), w_in[l], w_gla_a2[l], b_gla_a[l])
        o_gla = _gla(gq, gk, la, gv, gvT, gg, gla_norm_g[l])
        o_dsaT = _dsa(dq, iq, iwT, dk, dvT, ik2)
        x = _ffn_ln(x1, ffn2_w_gu[l], ffn2_w_down[l], ln3_g[l], ln3_b[l],
                    mix=(o_gla.reshape(n, GLA_WIDTH), o_dsaT, w_out[l], ln2_g[l], ln2_b[l]),
                    ).reshape(bsz, t, d)
    return x
```

```python
import functools

import jax
import jax.numpy as jnp
from jax import lax
from jax.experimental import pallas as pl
from jax.experimental.pallas import tpu as pltpu

F32 = jnp.float32
BF16 = jnp.bfloat16

D_MODEL = 1024
D_FF = 2816
CHUNK = 64
GLA_HEADS = 4
GLA_DK = 64
GLA_DV = 128
GLA_QK = GLA_HEADS * GLA_DK
GLA_WIDTH = GLA_HEADS * GLA_DV
GLA_LOWRANK = 16
GLA_TAU = 16.0
DSA_HEADS = 8
DSA_DH = 64
DSA_WIDTH = DSA_HEADS * DSA_DH
IDX_HEADS = 8
IDX_DIM = 64
TOPK_MAX = 256
LN_EPS = 1e-5
RMS_EPS = 1e-6
DEPTH = 1
DEEPNORM_ALPHA = (2.0 * DEPTH) ** 0.25
LOG2_E = 1.4426950408889634

LANES = 128
SUBLANES = 8
VMEM_LIMIT_BYTES = 56 * 1024 * 1024

FFN_TM = 512
FFN_FC = 256
PROJ_TM = 512
GLA_TT = 512
GLA_C = 128
GLA_MAX_CHUNK_DECAY = 60.0
GLA_SEQ_ROWS = 8
DSA_TQ = 256
DSA_TK = 256
NEG_BIG = -1e30
PACKED_SUBLANES = 16
SCORE_BIG = 3.0e38
SCORE_REL_STEP = 2.0 ** -20
SCORE_ABS_STEP = 1e-30
BISECT_STEPS = 30

_NT = (((1,), (1,)), ((), ()))


def _layer_norm(y, g, b):
    mu = jnp.mean(y, axis=-1, keepdims=True)
    yc = y - mu
    var = jnp.mean(yc * yc, axis=-1, keepdims=True)
    return yc * lax.rsqrt(var + LN_EPS) * g + b


def _silu(x):
    return x * jax.nn.sigmoid(x)


_TN = (((0,), (0,)), ((), ()))


def _ffn_ln_kernel(*refs, with_mix):
    if with_mix:
        (x_ref, og_ref, odT_ref, wog_ref, wod_ref, g0_ref, b0_ref,
         wgu_ref, wd_ref, g_ref, b_ref, o_ref, xb_ref, acc_ref) = refs
        mix = (jnp.dot(og_ref[...], wog_ref[...], preferred_element_type=F32)
               + lax.dot_general(odT_ref[0], wod_ref[...], _TN, preferred_element_type=F32))
        x = _layer_norm(DEEPNORM_ALPHA * x_ref[...] + mix, g0_ref[...], b0_ref[...])
    else:
        x_ref, wgu_ref, wd_ref, g_ref, b_ref, o_ref, xb_ref, acc_ref = refs
        x = x_ref[...]
    xb_ref[...] = x.astype(BF16)
    acc_ref[...] = jnp.zeros_like(acc_ref)

    for c in range(D_FF // FFN_FC):
        cols = slice(c * FFN_FC, (c + 1) * FFN_FC)
        xb = xb_ref[...]
        g = jnp.dot(xb, wgu_ref[:, cols], preferred_element_type=F32)
        u = jnp.dot(xb, wgu_ref[:, D_FF + c * FFN_FC:D_FF + (c + 1) * FFN_FC],
                    preferred_element_type=F32)
        a = (_silu(g) * u).astype(BF16)
        acc_ref[...] += jnp.dot(a, wd_ref[cols, :], preferred_element_type=F32)
    y = DEEPNORM_ALPHA * x + 0.5 * acc_ref[...]
    o_ref[...] = _layer_norm(y, g_ref[...], b_ref[...])


def _ffn_ln(x2d, w_gu, w_down, g, b, mix=None):
    n = x2d.shape[0]
    tm = FFN_TM
    const = lambda i: (0, 0)
    vec = pl.BlockSpec((1, D_MODEL), const)
    row = pl.BlockSpec((tm, D_MODEL), lambda i: (i, 0))
    operands, in_specs = [x2d], [row]
    if mix is not None:
        og, odT, w_out, g0, b0 = mix
        tpb = odT.shape[2] // tm
        operands += [og, odT, w_out[:GLA_WIDTH].astype(BF16), w_out[GLA_WIDTH:].astype(BF16),
                     g0.reshape(1, D_MODEL), b0.reshape(1, D_MODEL)]
        in_specs += [
            pl.BlockSpec((tm, GLA_WIDTH), lambda i: (i, 0)),
            pl.BlockSpec((1, DSA_WIDTH, tm), lambda i: (i // tpb, 0, i % tpb)),
            pl.BlockSpec((GLA_WIDTH, D_MODEL), const),
            pl.BlockSpec((DSA_WIDTH, D_MODEL), const),
            vec, vec,
        ]
    operands += [w_gu.astype(BF16), w_down.astype(BF16),
                 g.reshape(1, D_MODEL), b.reshape(1, D_MODEL)]
    in_specs += [
        pl.BlockSpec((D_MODEL, 2 * D_FF), const, pipeline_mode=pl.Buffered(1)),
        pl.BlockSpec((D_FF, D_MODEL), const, pipeline_mode=pl.Buffered(1)),
        vec, vec,
    ]
    return pl.pallas_call(
        functools.partial(_ffn_ln_kernel, with_mix=mix is not None),
        name="mix_ffn_ln" if mix is not None else "ffn_ln",
        grid=(n // tm,),
        in_specs=in_specs,
        out_specs=row,
        out_shape=jax.ShapeDtypeStruct((n, D_MODEL), F32),
        scratch_shapes=[
            pltpu.VMEM((tm, D_MODEL), BF16),
            pltpu.VMEM((tm, D_MODEL), F32),
        ],
        compiler_params=pltpu.CompilerParams(
            dimension_semantics=("arbitrary",), vmem_limit_bytes=VMEM_LIMIT_BYTES),
    )(*operands)


_N_GQ, _N_GK, _N_GV, _N_GG, _N_GA, _N_DQ, _N_DK, _N_IQ, _N_IK, _N_END = (
    0, 256, 512, 1024, 1536, 1664, 2176, 2688, 3200, 3328)
_T_DV, _T_IW, _T_END = 0, 512, 528


def _proj_kernel(x_ref, wn_ref, wt_ref, wa2_ref, ba_ref,
                 gq_ref, gk_ref, la_ref, gv_ref, gg_ref, dq_ref, dk_ref, iq_ref, ik_ref,
                 dvT_ref, iwT_ref):
    xb = x_ref[0].astype(BF16)

    def nproj(lo, hi):
        return jnp.dot(xb, wn_ref[:, lo:hi], preferred_element_type=F32)

    def tproj(lo, hi):
        return lax.dot_general(wt_ref[lo:hi, :], xb, _NT, preferred_element_type=F32)

    gq_ref[0] = nproj(_N_GQ, _N_GK)
    gk_ref[0] = nproj(_N_GK, _N_GV)
    gv_ref[0] = nproj(_N_GV, _N_GG).astype(BF16)
    gg_ref[0] = nproj(_N_GG, _N_GA)
    ga = nproj(_N_GA, _N_DQ)
    z = jnp.dot(ga.astype(BF16), wa2_ref[...], preferred_element_type=F32) + ba_ref[...]
    la_ref[0] = jax.nn.log_sigmoid(z) * (1.0 / GLA_TAU)
    dq_ref[0] = (nproj(_N_DQ, _N_DK) * (DSA_DH ** -0.5 * LOG2_E)).astype(BF16)
    dk_ref[0] = nproj(_N_DK, _N_IQ).astype(BF16)
    iq_ref[0] = (nproj(_N_IQ, _N_IK) * (IDX_DIM ** -0.5)).astype(BF16)
    ik_ref[0] = nproj(_N_IK, _N_END).astype(BF16)
    dvT = tproj(_T_DV, _T_IW).astype(BF16)
    for s in range(PROJ_TM // DSA_TK):
        dvT_ref[0, s] = dvT[:, s * DSA_TK:(s + 1) * DSA_TK]
    iwT_ref[0] = tproj(_T_IW, _T_END)[:IDX_HEADS, :] * (IDX_HEADS ** -0.5)


def _in_proj(x1, w_in, w_a2, b_a):
    bsz, t, _ = x1.shape
    o = [0, 256, 512, 1024, 1536, 1552, 2064, 2576, 3088, 3600, 3664, 3672]
    col = lambda i: w_in[:, o[i]:o[i + 1]]
    pad = lambda w, n: jnp.pad(w, ((0, 0), (0, n - w.shape[1])))
    wn = jnp.concatenate(
        [col(0), col(1), col(2), col(3), pad(col(4), LANES), col(5), col(6), col(8), col(9), col(9)],
        axis=1).astype(BF16)
    wt = jnp.concatenate([col(7), pad(col(10), 16)], axis=1).T.astype(BF16)
    wa2 = jnp.pad(w_a2, ((0, LANES - GLA_LOWRANK), (0, 0))).astype(BF16)
    ba = b_a.reshape(1, GLA_QK)
    tm = PROJ_TM
    spt = tm // DSA_TK
    row = lambda w: pl.BlockSpec((1, tm, w), lambda b, i: (b, i, 0))
    const2 = lambda b, i: (0, 0)
    out_shape = (
        jax.ShapeDtypeStruct((bsz, t, GLA_QK), F32),
        jax.ShapeDtypeStruct((bsz, t, GLA_QK), F32),
        jax.ShapeDtypeStruct((bsz, t, GLA_QK), F32),
        jax.ShapeDtypeStruct((bsz, t, GLA_WIDTH), BF16),
        jax.ShapeDtypeStruct((bsz, t, GLA_WIDTH), F32),
        jax.ShapeDtypeStruct((bsz, t, DSA_WIDTH), BF16),
        jax.ShapeDtypeStruct((bsz, t, DSA_WIDTH), BF16),
        jax.ShapeDtypeStruct((bsz, t, IDX_HEADS * IDX_DIM), BF16),
        jax.ShapeDtypeStruct((bsz, t, 2 * IDX_DIM), BF16),
        jax.ShapeDtypeStruct((bsz, t // DSA_TK, DSA_WIDTH, DSA_TK), BF16),
        jax.ShapeDtypeStruct((bsz, IDX_HEADS, t), F32),
    )
    out_specs = (
        row(GLA_QK), row(GLA_QK), row(GLA_QK), row(GLA_WIDTH), row(GLA_WIDTH),
        row(DSA_WIDTH), row(DSA_WIDTH), row(IDX_HEADS * IDX_DIM), row(2 * IDX_DIM),
        pl.BlockSpec((1, spt, DSA_WIDTH, DSA_TK), lambda b, i: (b, i, 0, 0)),
        pl.BlockSpec((1, IDX_HEADS, tm), lambda b, i: (b, 0, i)),
    )
    return pl.pallas_call(
        _proj_kernel,
        name="in_proj",
        grid=(bsz, t // tm),
        in_specs=[
            pl.BlockSpec((1, tm, D_MODEL), lambda b, i: (b, i, 0)),
            pl.BlockSpec((D_MODEL, _N_END), const2, pipeline_mode=pl.Buffered(1)),
            pl.BlockSpec((_T_END, D_MODEL), const2, pipeline_mode=pl.Buffered(1)),
            pl.BlockSpec((LANES, GLA_QK), const2),
            pl.BlockSpec((1, GLA_QK), const2),
        ],
        out_specs=out_specs,
        out_shape=out_shape,
        compiler_params=pltpu.CompilerParams(
            dimension_semantics=("arbitrary", "arbitrary"), vmem_limit_bytes=VMEM_LIMIT_BYTES),
    )(x1, wn, wt, wa2, ba)


def _gla_kernel(q_ref, k_ref, la_ref, v_ref, gg_ref, g_ref, o_ref, s_ref, of_ref):
    c = GLA_C

    @pl.when(pl.program_id(1) == 0)
    def _():
        s_ref[...] = jnp.zeros_like(s_ref)

    lane = lax.broadcasted_iota(jnp.int32, (c, LANES), 1)
    halves = (lane < GLA_DK, lane >= GLA_DK)
    gnorm = g_ref[...]

    def finish(o, rows, hl):
        ms = jnp.mean(o * o, axis=-1, keepdims=True)
        on = o * lax.rsqrt(ms + RMS_EPS) * gnorm
        return on * _silu(gg_ref[0, rows, hl])

    worst = jnp.zeros((1, GLA_QK), F32)
    for ch in range(GLA_TT // c):
        worst = jnp.minimum(
            worst, jnp.sum(la_ref[0, ch * c:(ch + 1) * c, :], axis=0, keepdims=True))
    mild = jnp.min(worst) >= -GLA_MAX_CHUNK_DECAY

    @pl.when(mild)
    def _():
        ri = lax.broadcasted_iota(jnp.int32, (c, c), 0)
        ci = lax.broadcasted_iota(jnp.int32, (c, c), 1)
        causal = ci <= ri
        tri = jnp.where(causal, 1.0, 0.0).astype(BF16)
        for ch in range(GLA_TT // c):
            rows = slice(ch * c, (ch + 1) * c)
            la = la_ref[0, rows, :]
            la_hi = la.astype(BF16)
            la_lo = (la - la_hi.astype(F32)).astype(BF16)
            b = (jnp.dot(tri, la_hi, preferred_element_type=F32)
                 + jnp.dot(tri, la_lo, preferred_element_type=F32))
            b_last = b[c - 1:c, :]
            q = q_ref[0, rows, :] * (GLA_DK ** -0.5)
            k = k_ref[0, rows, :]
            qd = (q * jnp.exp(b)).astype(BF16)
            kd = (k * jnp.exp(-b)).astype(BF16)
            kdl = (k * jnp.exp(b_last - b)).astype(BF16)
            dec = jnp.exp(b_last)
            for h in range(GLA_HEADS):
                p, hh = divmod(h, 2)
                pl_ = slice(p * LANES, (p + 1) * LANES)
                qd_p = qd[:, pl_]
                kd_m = jnp.where(halves[hh], kd[:, pl_], 0.0).astype(BF16)
                kdl_m = jnp.where(halves[hh], kdl[:, pl_], 0.0).astype(BF16)
                a = lax.dot_general(qd_p, kd_m, _NT, preferred_element_type=F32)
                a = jnp.where(causal, a, 0.0).astype(BF16)
                hl = slice(h * GLA_DV, (h + 1) * GLA_DV)
                o = jnp.dot(a, v_ref[0, rows, hl], preferred_element_type=F32)
                st = s_ref[h]
                o = o + lax.dot_general(qd_p, st.astype(BF16), _NT, preferred_element_type=F32)
                s_ref[h] = st * dec[:, pl_] + lax.dot_general(
                    v_ref[0, rows, hl], kdl_m, _TN, preferred_element_type=F32)
                o_ref[0, rows, hl] = finish(o, rows, hl).astype(o_ref.dtype)

    @pl.when(jnp.logical_not(mild))
    def _():
        nb = GLA_SEQ_ROWS
        tpos = lax.broadcasted_iota(jnp.int32, (GLA_TT, LANES), 0)
        half_row = (halves[0][:1], halves[1][:1])

        def block(tb, carry):
            r0 = pl.multiple_of(tb * nb, nb)
            qb = q_ref[0, pl.ds(r0, nb), :] * (GLA_DK ** -0.5)
            kb = k_ref[0, pl.ds(r0, nb), :]
            ab = jnp.exp(la_ref[0, pl.ds(r0, nb), :])
            outs = [[] for _ in range(GLA_HEADS)]
            for r in range(nb):
                pick = jnp.where(tpos == r0 + r, 1.0, 0.0).astype(BF16)
                for h in range(GLA_HEADS):
                    p, hh = divmod(h, 2)
                    pl_ = slice(p * LANES, (p + 1) * LANES)
                    hl = slice(h * GLA_DV, (h + 1) * GLA_DV)
                    vcol = lax.dot_general(v_ref[0, :, hl], pick, _TN,
                                           preferred_element_type=F32)
                    k_m = jnp.where(half_row[hh], kb[r:r + 1, pl_], 0.0)
                    st = s_ref[h] * ab[r:r + 1, pl_] + vcol * k_m
                    s_ref[h] = st
                    q_rep = jnp.broadcast_to(qb[r:r + 1, pl_], (PACKED_SUBLANES, LANES)).astype(BF16)
                    o = lax.dot_general(q_rep, st.astype(BF16), _NT, preferred_element_type=F32)
                    outs[h].append(o[0:1, :])
            for h in range(GLA_HEADS):
                of_ref[pl.ds(r0, nb), h * GLA_DV:(h + 1) * GLA_DV] = jnp.concatenate(outs[h], axis=0)
            return carry

        lax.fori_loop(0, GLA_TT // nb, block, 0)
        for h in range(GLA_HEADS):
            hl = slice(h * GLA_DV, (h + 1) * GLA_DV)
            o_ref[0, :, hl] = finish(of_ref[:, hl], slice(None), hl).astype(o_ref.dtype)


def _gla(gq, gk, la, gv, gg, gnorm):
    bsz, t, _ = gq.shape
    tt = GLA_TT
    row = lambda w: pl.BlockSpec((1, tt, w), lambda b, i: (b, i, 0))
    return pl.pallas_call(
        _gla_kernel,
        name="gla",
        grid=(bsz, t // tt),
        in_specs=[
            row(GLA_QK), row(GLA_QK), row(GLA_QK), row(GLA_WIDTH),
            row(GLA_WIDTH),
            pl.BlockSpec((1, GLA_DV), lambda b, i: (0, 0)),
        ],
        out_specs=row(GLA_WIDTH),
        out_shape=jax.ShapeDtypeStruct((bsz, t, GLA_WIDTH), BF16),
        scratch_shapes=[pltpu.VMEM((GLA_HEADS, GLA_DV, LANES), F32),
                        pltpu.VMEM((GLA_TT, GLA_WIDTH), F32)],
        compiler_params=pltpu.CompilerParams(
            dimension_semantics=("arbitrary", "arbitrary"), vmem_limit_bytes=VMEM_LIMIT_BYTES),
    )(gq, gk, la, gv, gg, gnorm.reshape(1, GLA_DV))


def _tile_pair_loop(nt, tile_fn, init):
    def body(i, carry):
        for u in range(2):
            carry = tile_fn(2 * i + u, carry)
        return carry
    return lax.fori_loop(0, lax.shift_right_logical(nt + 1, 1), body, init)


def _dsa_kernel(q_ref, qi_ref, wT_ref, k_ref, vT_ref, ki_ref, oT_ref,
                score_ref, bias_ref, qm_ref, qim_ref, s_ref, p_ref,
                m_ref, al_ref, acc_ref, *, topk):
    tq, tk = DSA_TQ, DSA_TK
    j = pl.program_id(1)
    nt = j + 1
    q0 = j * tq

    lane = lax.broadcasted_iota(jnp.int32, (tq, LANES), 1)
    halves = (lane < DSA_DH, lane >= DSA_DH)
    qcol = q0 + lax.broadcasted_iota(jnp.int32, (1, tq), 1)
    lim = (qcol // CHUNK + 1) * CHUNK
    krow = lax.broadcasted_iota(jnp.int32, (tk, tq), 0)

    for h in range(DSA_HEADS):
        p, hh = divmod(h, 2)
        pl_ = slice(p * LANES, (p + 1) * LANES)
        qm_ref[h] = jnp.where(halves[hh], q_ref[0, :, pl_], 0.0).astype(BF16)
        qim_ref[h] = jnp.where(halves[hh], qi_ref[0, :, pl_], 0.0).astype(BF16)

    wT = wT_ref[0]

    def score_tile(t, carry):
        top, bot = carry
        r0 = pl.multiple_of(t * tk, tk)
        ki2 = ki_ref[0, pl.ds(r0, tk), :]
        acc = jnp.zeros((tk, tq), F32)
        for h in range(IDX_HEADS):
            s = lax.dot_general(ki2, qim_ref[h], _NT, preferred_element_type=F32)
            acc = acc + wT[h:h + 1, :] * jnp.maximum(s, 0.0)
        adm = krow + r0 < lim
        score_ref[t] = jnp.where(adm, acc, -jnp.inf)
        top = jnp.maximum(top, jnp.max(jnp.where(adm, acc, -SCORE_BIG), axis=0, keepdims=True))
        bot = jnp.minimum(bot, jnp.min(jnp.where(adm, acc, SCORE_BIG), axis=0, keepdims=True))
        return top, bot

    top, bot = _tile_pair_loop(
        nt, score_tile,
        (jnp.full((1, tq), -SCORE_BIG, F32), jnp.full((1, tq), SCORE_BIG, F32)))

    def count_ge(cand):
        def tile(t, acc):
            hit = jnp.where(score_ref[t] >= cand, 1.0, 0.0)
            parts = [hit[g * SUBLANES:(g + 1) * SUBLANES, :] for g in range(tk // SUBLANES)]
            while len(parts) > 1:
                parts = [a + b for a, b in zip(parts[::2], parts[1::2])]
            return acc + parts[0]
        acc = _tile_pair_loop(nt, tile, jnp.zeros((SUBLANES, tq), F32))
        return jnp.sum(acc, axis=0, keepdims=True)

    def bisect(i, state):
        lo, hi, n_lo = state
        mid = 0.5 * lo + 0.5 * hi
        c = count_ge(mid)
        ok = c >= topk
        return jnp.where(ok, mid, lo), jnp.where(ok, hi, mid), jnp.where(ok, c, n_lo)

    hi0 = top + (jnp.abs(top) * SCORE_REL_STEP + SCORE_ABS_STEP)
    lo, hi, n_lo = lax.fori_loop(0, BISECT_STEPS, bisect, (bot, hi0, lim.astype(F32)))
    few = lim <= topk
    lo = jnp.where(few, -SCORE_BIG, lo)
    excess = jnp.where(few, 0.0, n_lo - topk)
    has_tie = jnp.max(jnp.where(excess > 0.0, 1, 0))

    @pl.when(has_tie == 0)
    def _():
        def tile(t, carry):
            bias_ref[t] = jnp.where(score_ref[t] >= lo, 0.0, NEG_BIG)
            return carry
        _tile_pair_loop(nt, tile, 0)

    @pl.when(has_tie > 0)
    def _():
        ri = lax.broadcasted_iota(jnp.int32, (tk, tk), 0)
        ci = lax.broadcasted_iota(jnp.int32, (tk, tk), 1)
        later = jnp.where(ci > ri, 1.0, 0.0).astype(BF16)

        def tile(t, after):
            sc = score_ref[t]
            ones = jnp.where(sc >= lo, jnp.where(sc < hi, 1.0, 0.0), 0.0)
            behind = after + jnp.dot(later, ones.astype(BF16), preferred_element_type=F32)
            tie_bias = jnp.where(behind >= excess, 0.0, NEG_BIG)
            bias_ref[t] = jnp.where(sc >= hi, 0.0, jnp.where(ones > 0.0, tie_bias, NEG_BIG))
            return behind[0:1, :] + ones[0:1, :]

        n_pairs = lax.shift_right_logical(nt + 1, 1)

        def pair(i, after):
            p = n_pairs - 1 - i
            return tile(2 * p, tile(2 * p + 1, after))
        lax.fori_loop(0, n_pairs, pair, jnp.zeros((1, tq), F32))

    m_ref[...] = jnp.full(m_ref.shape, NEG_BIG, F32)
    acc_ref[...] = jnp.zeros(acc_ref.shape, F32)

    def att_tile(t, carry):
        r0 = pl.multiple_of(t * tk, tk)
        bias = bias_ref[t]
        for h in range(DSA_HEADS):
            p = h // 2
            kt = k_ref[0, pl.ds(r0, tk), p * LANES:(p + 1) * LANES]
            s = lax.dot_general(kt, qm_ref[h], _NT, preferred_element_type=F32) + bias
            s_ref[h] = s
            al_ref[h:h + 1, :] = jnp.max(s, axis=0, keepdims=True)
        for h in range(DSA_HEADS):
            m_old = m_ref[h:h + 1, :]
            m_new = jnp.maximum(m_old, al_ref[h:h + 1, :])
            p_ref[h] = jnp.exp2(s_ref[h] - m_new).astype(BF16)
            m_ref[h:h + 1, :] = m_new
            al_ref[h:h + 1, :] = jnp.exp2(m_old - m_new)
        ones = jnp.ones((PACKED_SUBLANES, tk), BF16)
        for h in range(DSA_HEADS):
            vt = jnp.concatenate([vT_ref[0, t, h * DSA_DH:(h + 1) * DSA_DH, :], ones], axis=0)
            acc_ref[h] = al_ref[h:h + 1, :] * acc_ref[h] + jnp.dot(
                vt, p_ref[h], preferred_element_type=F32)
        return carry

    lax.fori_loop(0, nt, att_tile, 0)
    for h in range(DSA_HEADS):
        oT_ref[0, h * DSA_DH:(h + 1) * DSA_DH, :] = (
            acc_ref[h, :DSA_DH, :] / acc_ref[h, DSA_DH:DSA_DH + 1, :]).astype(oT_ref.dtype)


def _dsa(dq, iq, iwT, dk, dvT, ik2):
    bsz, t, _ = dq.shape
    tq, tk = DSA_TQ, DSA_TK
    assert tq == tk and t % tq == 0
    nt = t // tk
    assert nt % 2 == 0
    topk = min(TOPK_MAX, t // 4)
    row = lambda w: pl.BlockSpec((1, tq, w), lambda b, i: (b, i, 0))
    return pl.pallas_call(
        functools.partial(_dsa_kernel, topk=topk),
        name="dsa",
        grid=(bsz, t // tq),
        in_specs=[
            row(DSA_WIDTH), row(IDX_HEADS * IDX_DIM),
            pl.BlockSpec((1, IDX_HEADS, tq), lambda b, i: (b, 0, i)),
            pl.BlockSpec((1, t, DSA_WIDTH), lambda b, i: (b, 0, 0)),
            pl.BlockSpec((1, nt, DSA_WIDTH, tk), lambda b, i: (b, 0, 0, 0)),
            pl.BlockSpec((1, t, 2 * IDX_DIM), lambda b, i: (b, 0, 0)),
        ],
        out_specs=pl.BlockSpec((1, DSA_WIDTH, tq), lambda b, i: (b, 0, i)),
        out_shape=jax.ShapeDtypeStruct((bsz, DSA_WIDTH, t), BF16),
        scratch_shapes=[
            pltpu.VMEM((nt, tk, tq), F32),
            pltpu.VMEM((nt, tk, tq), F32),
            pltpu.VMEM((DSA_HEADS, tq, LANES), BF16),
            pltpu.VMEM((IDX_HEADS, tq, LANES), BF16),
            pltpu.VMEM((DSA_HEADS, tk, tq), F32),
            pltpu.VMEM((DSA_HEADS, tk, tq), BF16),
            pltpu.VMEM((DSA_HEADS, tq), F32),
            pltpu.VMEM((DSA_HEADS, tq), F32),
            pltpu.VMEM((DSA_HEADS, DSA_DH + PACKED_SUBLANES, tq), F32),
        ],
        compiler_params=pltpu.CompilerParams(
            dimension_semantics=("arbitrary", "arbitrary"), vmem_limit_bytes=VMEM_LIMIT_BYTES),
    )(dq, iq, iwT, dk, dvT, ik2)


def kernel(x, w_in, w_gla_a2, b_gla_a, gla_norm_g, w_out, ffn1_w_gu, ffn1_w_down,
           ffn2_w_gu, ffn2_w_down, ln1_g, ln1_b, ln2_g, ln2_b, ln3_g, ln3_b):
    bsz, t, d = x.shape
    n = bsz * t
    for l in range(DEPTH):
        x1 = _ffn_ln(x.reshape(n, d), ffn1_w_gu[l], ffn1_w_down[l], ln1_g[l], ln1_b[l])
        gq, gk, la, gv, gg, dq, dk, iq, ik2, dvT, iwT = _in_proj(
            x1.reshape(bsz, t, d), w_in[l], w_gla_a2[l], b_gla_a[l])
        o_gla = _gla(gq, gk, la, gv, gg, gla_norm_g[l])
        o_dsaT = _dsa(dq, iq, iwT, dk, dvT, ik2)
        x = _ffn_ln(x1, ffn2_w_gu[l], ffn2_w_down[l], ln3_g[l], ln3_b[l],
                    mix=(o_gla.reshape(n, GLA_WIDTH), o_dsaT, w_out[l], ln2_g[l], ln2_b[l]),
                    ).reshape(bsz, t, d)
    return x
```

```python
import functools

import jax
import jax.numpy as jnp
from jax import lax
from jax.experimental import pallas as pl
from jax.experimental.pallas import tpu as pltpu

F32 = jnp.float32
BF16 = jnp.bfloat16

D_MODEL = 1024
D_FF = 2816
CHUNK = 64
GLA_HEADS = 4
GLA_DK = 64
GLA_DV = 128
GLA_QK = GLA_HEADS * GLA_DK
GLA_WIDTH = GLA_HEADS * GLA_DV
GLA_LOWRANK = 16
GLA_TAU = 16.0
DSA_HEADS = 8
DSA_DH = 64
DSA_WIDTH = DSA_HEADS * DSA_DH
IDX_HEADS = 8
IDX_DIM = 64
TOPK_MAX = 256
LN_EPS = 1e-5
RMS_EPS = 1e-6
DEPTH = 1
DEEPNORM_ALPHA = (2.0 * DEPTH) ** 0.25
LOG2_E = 1.4426950408889634

LANES = 128
SUBLANES = 8
VMEM_LIMIT_BYTES = 56 * 1024 * 1024

FFN_TM = 512
FFN_FC = 256
PROJ_TM = 512
GLA_TT = 512
GLA_C = 128
GLA_MAX_CHUNK_DECAY = 60.0
GLA_SEQ_ROWS = 8
DSA_TQ = 256
DSA_TK = 256
NEG_BIG = -1e30
PACKED_SUBLANES = 16
SCORE_BIG = 3.0e38
SCORE_ABS_STEP = 1e-30
BF16_SLACK = 2.0 ** -6
COARSE_STEPS = 14
FINE_STEPS = 18

_NT = (((1,), (1,)), ((), ()))


def _layer_norm(y, g, b):
    mu = jnp.mean(y, axis=-1, keepdims=True)
    yc = y - mu
    var = jnp.mean(yc * yc, axis=-1, keepdims=True)
    return yc * lax.rsqrt(var + LN_EPS) * g + b


def _silu(x):
    return x * jax.nn.sigmoid(x)


_TN = (((0,), (0,)), ((), ()))


def _ffn_ln_kernel(*refs, with_mix):
    if with_mix:
        (x_ref, og_ref, odT_ref, wog_ref, wod_ref, g0_ref, b0_ref,
         wgu_ref, wd_ref, g_ref, b_ref, o_ref, xb_ref, acc_ref) = refs
        mix = (jnp.dot(og_ref[...], wog_ref[...], preferred_element_type=F32)
               + lax.dot_general(odT_ref[0], wod_ref[...], _TN, preferred_element_type=F32))
        x = _layer_norm(DEEPNORM_ALPHA * x_ref[...] + mix, g0_ref[...], b0_ref[...])
    else:
        x_ref, wgu_ref, wd_ref, g_ref, b_ref, o_ref, xb_ref, acc_ref = refs
        x = x_ref[...]
    xb_ref[...] = x.astype(BF16)
    acc_ref[...] = jnp.zeros_like(acc_ref)

    for c in range(D_FF // FFN_FC):
        cols = slice(c * FFN_FC, (c + 1) * FFN_FC)
        xb = xb_ref[...]
        g = jnp.dot(xb, wgu_ref[:, cols], preferred_element_type=F32)
        u = jnp.dot(xb, wgu_ref[:, D_FF + c * FFN_FC:D_FF + (c + 1) * FFN_FC],
                    preferred_element_type=F32)
        a = (_silu(g) * u).astype(BF16)
        acc_ref[...] += jnp.dot(a, wd_ref[cols, :], preferred_element_type=F32)
    y = DEEPNORM_ALPHA * x + 0.5 * acc_ref[...]
    o_ref[...] = _layer_norm(y, g_ref[...], b_ref[...])


def _ffn_ln(x2d, w_gu, w_down, g, b, mix=None):
    n = x2d.shape[0]
    tm = FFN_TM
    const = lambda i: (0, 0)
    vec = pl.BlockSpec((1, D_MODEL), const)
    row = pl.BlockSpec((tm, D_MODEL), lambda i: (i, 0))
    operands, in_specs = [x2d], [row]
    if mix is not None:
        og, odT, w_out, g0, b0 = mix
        tpb = odT.shape[2] // tm
        operands += [og, odT, w_out[:GLA_WIDTH].astype(BF16), w_out[GLA_WIDTH:].astype(BF16),
                     g0.reshape(1, D_MODEL), b0.reshape(1, D_MODEL)]
        in_specs += [
            pl.BlockSpec((tm, GLA_WIDTH), lambda i: (i, 0)),
            pl.BlockSpec((1, DSA_WIDTH, tm), lambda i: (i // tpb, 0, i % tpb)),
            pl.BlockSpec((GLA_WIDTH, D_MODEL), const),
            pl.BlockSpec((DSA_WIDTH, D_MODEL), const),
            vec, vec,
        ]
    operands += [w_gu.astype(BF16), w_down.astype(BF16),
                 g.reshape(1, D_MODEL), b.reshape(1, D_MODEL)]
    in_specs += [
        pl.BlockSpec((D_MODEL, 2 * D_FF), const, pipeline_mode=pl.Buffered(1)),
        pl.BlockSpec((D_FF, D_MODEL), const, pipeline_mode=pl.Buffered(1)),
        vec, vec,
    ]
    return pl.pallas_call(
        functools.partial(_ffn_ln_kernel, with_mix=mix is not None),
        name="mix_ffn_ln" if mix is not None else "ffn_ln",
        grid=(n // tm,),
        in_specs=in_specs,
        out_specs=row,
        out_shape=jax.ShapeDtypeStruct((n, D_MODEL), F32),
        scratch_shapes=[
            pltpu.VMEM((tm, D_MODEL), BF16),
            pltpu.VMEM((tm, D_MODEL), F32),
        ],
        compiler_params=pltpu.CompilerParams(
            dimension_semantics=("arbitrary",), vmem_limit_bytes=VMEM_LIMIT_BYTES),
    )(*operands)


_N_GQ, _N_GK, _N_GV, _N_GG, _N_GA, _N_DQ, _N_DK, _N_IQ, _N_IK, _N_END = (
    0, 256, 512, 1024, 1536, 1664, 2176, 2688, 3200, 3328)
_T_DV, _T_IW, _T_END = 0, 512, 528


def _proj_kernel(x_ref, wn_ref, wt_ref, wa2_ref, ba_ref,
                 gq_ref, gk_ref, la_ref, gv_ref, gg_ref, dq_ref, dk_ref, iq_ref, ik_ref,
                 dvT_ref, iwT_ref):
    xb = x_ref[0].astype(BF16)

    def nproj(lo, hi):
        return jnp.dot(xb, wn_ref[:, lo:hi], preferred_element_type=F32)

    def tproj(lo, hi):
        return lax.dot_general(wt_ref[lo:hi, :], xb, _NT, preferred_element_type=F32)

    gq_ref[0] = nproj(_N_GQ, _N_GK)
    gk_ref[0] = nproj(_N_GK, _N_GV)
    gv_ref[0] = nproj(_N_GV, _N_GG).astype(BF16)
    gg_ref[0] = nproj(_N_GG, _N_GA)
    ga = nproj(_N_GA, _N_DQ)
    z = jnp.dot(ga.astype(BF16), wa2_ref[...], preferred_element_type=F32) + ba_ref[...]
    la_ref[0] = jax.nn.log_sigmoid(z) * (1.0 / GLA_TAU)
    dq_ref[0] = (nproj(_N_DQ, _N_DK) * (DSA_DH ** -0.5 * LOG2_E)).astype(BF16)
    dk_ref[0] = nproj(_N_DK, _N_IQ).astype(BF16)
    iq_ref[0] = (nproj(_N_IQ, _N_IK) * (IDX_DIM ** -0.5)).astype(BF16)
    ik_ref[0] = nproj(_N_IK, _N_END).astype(BF16)
    dvT = tproj(_T_DV, _T_IW).astype(BF16)
    for s in range(PROJ_TM // DSA_TK):
        dvT_ref[0, s] = dvT[:, s * DSA_TK:(s + 1) * DSA_TK]
    iwT_ref[0] = tproj(_T_IW, _T_END)[:IDX_HEADS, :] * (IDX_HEADS ** -0.5)


def _in_proj(x1, w_in, w_a2, b_a):
    bsz, t, _ = x1.shape
    o = [0, 256, 512, 1024, 1536, 1552, 2064, 2576, 3088, 3600, 3664, 3672]
    col = lambda i: w_in[:, o[i]:o[i + 1]]
    pad = lambda w, n: jnp.pad(w, ((0, 0), (0, n - w.shape[1])))
    wn = jnp.concatenate(
        [col(0), col(1), col(2), col(3), pad(col(4), LANES), col(5), col(6), col(8), col(9), col(9)],
        axis=1).astype(BF16)
    wt = jnp.concatenate([col(7), pad(col(10), 16)], axis=1).T.astype(BF16)
    wa2 = jnp.pad(w_a2, ((0, LANES - GLA_LOWRANK), (0, 0))).astype(BF16)
    ba = b_a.reshape(1, GLA_QK)
    tm = PROJ_TM
    spt = tm // DSA_TK
    row = lambda w: pl.BlockSpec((1, tm, w), lambda b, i: (b, i, 0))
    const2 = lambda b, i: (0, 0)
    out_shape = (
        jax.ShapeDtypeStruct((bsz, t, GLA_QK), F32),
        jax.ShapeDtypeStruct((bsz, t, GLA_QK), F32),
        jax.ShapeDtypeStruct((bsz, t, GLA_QK), F32),
        jax.ShapeDtypeStruct((bsz, t, GLA_WIDTH), BF16),
        jax.ShapeDtypeStruct((bsz, t, GLA_WIDTH), F32),
        jax.ShapeDtypeStruct((bsz, t, DSA_WIDTH), BF16),
        jax.ShapeDtypeStruct((bsz, t, DSA_WIDTH), BF16),
        jax.ShapeDtypeStruct((bsz, t, IDX_HEADS * IDX_DIM), BF16),
        jax.ShapeDtypeStruct((bsz, t, 2 * IDX_DIM), BF16),
        jax.ShapeDtypeStruct((bsz, t // DSA_TK, DSA_WIDTH, DSA_TK), BF16),
        jax.ShapeDtypeStruct((bsz, IDX_HEADS, t), F32),
    )
    out_specs = (
        row(GLA_QK), row(GLA_QK), row(GLA_QK), row(GLA_WIDTH), row(GLA_WIDTH),
        row(DSA_WIDTH), row(DSA_WIDTH), row(IDX_HEADS * IDX_DIM), row(2 * IDX_DIM),
        pl.BlockSpec((1, spt, DSA_WIDTH, DSA_TK), lambda b, i: (b, i, 0, 0)),
        pl.BlockSpec((1, IDX_HEADS, tm), lambda b, i: (b, 0, i)),
    )
    return pl.pallas_call(
        _proj_kernel,
        name="in_proj",
        grid=(bsz, t // tm),
        in_specs=[
            pl.BlockSpec((1, tm, D_MODEL), lambda b, i: (b, i, 0)),
            pl.BlockSpec((D_MODEL, _N_END), const2, pipeline_mode=pl.Buffered(1)),
            pl.BlockSpec((_T_END, D_MODEL), const2, pipeline_mode=pl.Buffered(1)),
            pl.BlockSpec((LANES, GLA_QK), const2),
            pl.BlockSpec((1, GLA_QK), const2),
        ],
        out_specs=out_specs,
        out_shape=out_shape,
        compiler_params=pltpu.CompilerParams(
            dimension_semantics=("arbitrary", "arbitrary"), vmem_limit_bytes=VMEM_LIMIT_BYTES),
    )(x1, wn, wt, wa2, ba)


def _gla_kernel(q_ref, k_ref, la_ref, v_ref, gg_ref, g_ref, o_ref, s_ref, of_ref):
    c = GLA_C

    @pl.when(pl.program_id(1) == 0)
    def _():
        s_ref[...] = jnp.zeros_like(s_ref)

    lane = lax.broadcasted_iota(jnp.int32, (c, LANES), 1)
    halves = (lane < GLA_DK, lane >= GLA_DK)
    gnorm = g_ref[...]

    def finish(o, rows, hl):
        ms = jnp.mean(o * o, axis=-1, keepdims=True)
        on = o * lax.rsqrt(ms + RMS_EPS) * gnorm
        return on * _silu(gg_ref[0, rows, hl])

    worst = jnp.zeros((1, GLA_QK), F32)
    for ch in range(GLA_TT // c):
        worst = jnp.minimum(
            worst, jnp.sum(la_ref[0, ch * c:(ch + 1) * c, :], axis=0, keepdims=True))
    mild = jnp.min(worst) >= -GLA_MAX_CHUNK_DECAY

    @pl.when(mild)
    def _():
        ri = lax.broadcasted_iota(jnp.int32, (c, c), 0)
        ci = lax.broadcasted_iota(jnp.int32, (c, c), 1)
        causal = ci <= ri
        tri = jnp.where(causal, 1.0, 0.0).astype(BF16)
        for ch in range(GLA_TT // c):
            rows = slice(ch * c, (ch + 1) * c)
            la = la_ref[0, rows, :]
            la_hi = la.astype(BF16)
            la_lo = (la - la_hi.astype(F32)).astype(BF16)
            b = (jnp.dot(tri, la_hi, preferred_element_type=F32)
                 + jnp.dot(tri, la_lo, preferred_element_type=F32))
            b_last = b[c - 1:c, :]
            q = q_ref[0, rows, :] * (GLA_DK ** -0.5)
            k = k_ref[0, rows, :]
            qd = (q * jnp.exp(b)).astype(BF16)
            kd = (k * jnp.exp(-b)).astype(BF16)
            kdl = (k * jnp.exp(b_last - b)).astype(BF16)
            dec = jnp.exp(b_last)
            for h in range(GLA_HEADS):
                p, hh = divmod(h, 2)
                pl_ = slice(p * LANES, (p + 1) * LANES)
                qd_p = qd[:, pl_]
                kd_m = jnp.where(halves[hh], kd[:, pl_], 0.0).astype(BF16)
                kdl_m = jnp.where(halves[hh], kdl[:, pl_], 0.0).astype(BF16)
                a = lax.dot_general(qd_p, kd_m, _NT, preferred_element_type=F32)
                a = jnp.where(causal, a, 0.0).astype(BF16)
                hl = slice(h * GLA_DV, (h + 1) * GLA_DV)
                o = jnp.dot(a, v_ref[0, rows, hl], preferred_element_type=F32)
                st = s_ref[h]
                o = o + lax.dot_general(qd_p, st.astype(BF16), _NT, preferred_element_type=F32)
                s_ref[h] = st * dec[:, pl_] + lax.dot_general(
                    v_ref[0, rows, hl], kdl_m, _TN, preferred_element_type=F32)
                o_ref[0, rows, hl] = finish(o, rows, hl).astype(o_ref.dtype)

    @pl.when(jnp.logical_not(mild))
    def _():
        nb = GLA_SEQ_ROWS
        tpos = lax.broadcasted_iota(jnp.int32, (GLA_TT, LANES), 0)
        half_row = (halves[0][:1], halves[1][:1])

        def block(tb, carry):
            r0 = pl.multiple_of(tb * nb, nb)
            qb = q_ref[0, pl.ds(r0, nb), :] * (GLA_DK ** -0.5)
            kb = k_ref[0, pl.ds(r0, nb), :]
            ab = jnp.exp(la_ref[0, pl.ds(r0, nb), :])
            outs = [[] for _ in range(GLA_HEADS)]
            for r in range(nb):
                pick = jnp.where(tpos == r0 + r, 1.0, 0.0).astype(BF16)
                for h in range(GLA_HEADS):
                    p, hh = divmod(h, 2)
                    pl_ = slice(p * LANES, (p + 1) * LANES)
                    hl = slice(h * GLA_DV, (h + 1) * GLA_DV)
                    vcol = lax.dot_general(v_ref[0, :, hl], pick, _TN,
                                           preferred_element_type=F32)
                    k_m = jnp.where(half_row[hh], kb[r:r + 1, pl_], 0.0)
                    st = s_ref[h] * ab[r:r + 1, pl_] + vcol * k_m
                    s_ref[h] = st
                    q_rep = jnp.broadcast_to(qb[r:r + 1, pl_], (PACKED_SUBLANES, LANES)).astype(BF16)
                    o = lax.dot_general(q_rep, st.astype(BF16), _NT, preferred_element_type=F32)
                    outs[h].append(o[0:1, :])
            for h in range(GLA_HEADS):
                of_ref[pl.ds(r0, nb), h * GLA_DV:(h + 1) * GLA_DV] = jnp.concatenate(outs[h], axis=0)
            return carry

        lax.fori_loop(0, GLA_TT // nb, block, 0)
        for h in range(GLA_HEADS):
            hl = slice(h * GLA_DV, (h + 1) * GLA_DV)
            o_ref[0, :, hl] = finish(of_ref[:, hl], slice(None), hl).astype(o_ref.dtype)


def _gla(gq, gk, la, gv, gg, gnorm):
    bsz, t, _ = gq.shape
    tt = GLA_TT
    row = lambda w: pl.BlockSpec((1, tt, w), lambda b, i: (b, i, 0))
    return pl.pallas_call(
        _gla_kernel,
        name="gla",
        grid=(bsz, t // tt),
        in_specs=[
            row(GLA_QK), row(GLA_QK), row(GLA_QK), row(GLA_WIDTH),
            row(GLA_WIDTH),
            pl.BlockSpec((1, GLA_DV), lambda b, i: (0, 0)),
        ],
        out_specs=row(GLA_WIDTH),
        out_shape=jax.ShapeDtypeStruct((bsz, t, GLA_WIDTH), BF16),
        scratch_shapes=[pltpu.VMEM((GLA_HEADS, GLA_DV, LANES), F32),
                        pltpu.VMEM((GLA_TT, GLA_WIDTH), F32)],
        compiler_params=pltpu.CompilerParams(
            dimension_semantics=("arbitrary", "arbitrary"), vmem_limit_bytes=VMEM_LIMIT_BYTES),
    )(gq, gk, la, gv, gg, gnorm.reshape(1, GLA_DV))


def _tile_pair_loop(nt, tile_fn, init):
    def body(i, carry):
        for u in range(2):
            carry = tile_fn(2 * i + u, carry)
        return carry
    return lax.fori_loop(0, lax.shift_right_logical(nt + 1, 1), body, init)


def _dsa_kernel(q_ref, qi_ref, wT_ref, k_ref, vT_ref, ki_ref, oT_ref,
                score_ref, sb_ref, bias_ref, qm_ref, qim_ref, s_ref, p_ref,
                m_ref, al_ref, acc_ref, *, topk):
    tq, tk = DSA_TQ, DSA_TK
    j = pl.program_id(1)
    nt = j + 1
    q0 = j * tq

    lane = lax.broadcasted_iota(jnp.int32, (tq, LANES), 1)
    halves = (lane < DSA_DH, lane >= DSA_DH)
    qcol = q0 + lax.broadcasted_iota(jnp.int32, (1, tq), 1)
    lim = (qcol // CHUNK + 1) * CHUNK
    krow = lax.broadcasted_iota(jnp.int32, (tk, tq), 0)

    for h in range(DSA_HEADS):
        p, hh = divmod(h, 2)
        pl_ = slice(p * LANES, (p + 1) * LANES)
        qm_ref[h] = jnp.where(halves[hh], q_ref[0, :, pl_], 0.0).astype(BF16)
        qim_ref[h] = jnp.where(halves[hh], qi_ref[0, :, pl_], 0.0).astype(BF16)

    wT = wT_ref[0]

    def score_tile(t, carry):
        top, bot = carry
        r0 = pl.multiple_of(t * tk, tk)
        ki2 = ki_ref[0, pl.ds(r0, tk), :]
        acc = jnp.zeros((tk, tq), F32)
        for h in range(IDX_HEADS):
            s = lax.dot_general(ki2, qim_ref[h], _NT, preferred_element_type=F32)
            acc = acc + wT[h:h + 1, :] * jnp.maximum(s, 0.0)
        adm = krow + r0 < lim
        sc = jnp.where(adm, acc, -jnp.inf)
        score_ref[t] = sc
        sb_ref[t] = sc.astype(BF16)
        top = jnp.maximum(top, jnp.max(jnp.where(adm, acc, -SCORE_BIG), axis=0, keepdims=True))
        bot = jnp.minimum(bot, jnp.min(jnp.where(adm, acc, SCORE_BIG), axis=0, keepdims=True))
        return top, bot

    top, bot = _tile_pair_loop(
        nt, score_tile,
        (jnp.full((1, tq), -SCORE_BIG, F32), jnp.full((1, tq), SCORE_BIG, F32)))

    def count_ge(cand):
        def tile(t, acc):
            hit = jnp.where(score_ref[t] >= cand, 1.0, 0.0)
            parts = [hit[g * SUBLANES:(g + 1) * SUBLANES, :] for g in range(tk // SUBLANES)]
            while len(parts) > 1:
                parts = [a + b for a, b in zip(parts[::2], parts[1::2])]
            return acc + parts[0]
        acc = _tile_pair_loop(nt, tile, jnp.zeros((SUBLANES, tq), F32))
        return jnp.sum(acc, axis=0, keepdims=True)

    def count_ge_coarse(cand):
        def tile(t, acc):
            hit = jnp.where(sb_ref[t] >= cand, jnp.ones((), BF16), jnp.zeros((), BF16))
            parts = [hit[g * PACKED_SUBLANES:(g + 1) * PACKED_SUBLANES, :]
                     for g in range(tk // PACKED_SUBLANES)]
            while len(parts) > 1:
                parts = [a + b for a, b in zip(parts[::2], parts[1::2])]
            return acc + parts[0]
        acc = _tile_pair_loop(nt, tile, jnp.zeros((PACKED_SUBLANES, tq), BF16))
        return jnp.sum(acc.astype(F32), axis=0, keepdims=True)

    def slack(x):
        return jnp.abs(x) * BF16_SLACK + SCORE_ABS_STEP

    def coarse(i, state):
        lo, hi = state
        mid = (0.5 * lo + 0.5 * hi).astype(BF16)
        ok = count_ge_coarse(mid) >= topk
        mid = mid.astype(F32)
        return jnp.where(ok, mid, lo), jnp.where(ok, hi, mid)

    def bisect(i, state):
        lo, hi, n_lo = state
        mid = 0.5 * lo + 0.5 * hi
        c = count_ge(mid)
        ok = c >= topk
        return jnp.where(ok, mid, lo), jnp.where(ok, hi, mid), jnp.where(ok, c, n_lo)

    lo, hi = lax.fori_loop(
        0, COARSE_STEPS, coarse,
        ((bot - slack(bot)).astype(BF16).astype(F32), (top + slack(top)).astype(BF16).astype(F32)))
    lo = lo - slack(lo)
    hi = hi + slack(hi)
    lo, hi, n_lo = lax.fori_loop(0, FINE_STEPS, bisect, (lo, hi, count_ge(lo)))
    few = lim <= topk
    lo = jnp.where(few, -SCORE_BIG, lo)
    excess = jnp.where(few, 0.0, n_lo - topk)
    has_tie = jnp.max(jnp.where(excess > 0.0, 1, 0))

    @pl.when(has_tie == 0)
    def _():
        def tile(t, carry):
            bias_ref[t] = jnp.where(score_ref[t] >= lo, 0.0, NEG_BIG)
            return carry
        _tile_pair_loop(nt, tile, 0)

    @pl.when(has_tie > 0)
    def _():
        ri = lax.broadcasted_iota(jnp.int32, (tk, tk), 0)
        ci = lax.broadcasted_iota(jnp.int32, (tk, tk), 1)
        later = jnp.where(ci > ri, 1.0, 0.0).astype(BF16)

        def tile(t, after):
            sc = score_ref[t]
            ones = jnp.where(sc >= lo, jnp.where(sc < hi, 1.0, 0.0), 0.0)
            behind = after + jnp.dot(later, ones.astype(BF16), preferred_element_type=F32)
            tie_bias = jnp.where(behind >= excess, 0.0, NEG_BIG)
            bias_ref[t] = jnp.where(sc >= hi, 0.0, jnp.where(ones > 0.0, tie_bias, NEG_BIG))
            return behind[0:1, :] + ones[0:1, :]

        n_pairs = lax.shift_right_logical(nt + 1, 1)

        def pair(i, after):
            p = n_pairs - 1 - i
            return tile(2 * p, tile(2 * p + 1, after))
        lax.fori_loop(0, n_pairs, pair, jnp.zeros((1, tq), F32))

    m_ref[...] = jnp.full(m_ref.shape, NEG_BIG, F32)
    acc_ref[...] = jnp.zeros(acc_ref.shape, F32)

    def att_tile(t, carry):
        r0 = pl.multiple_of(t * tk, tk)
        bias = bias_ref[t]
        for h in range(DSA_HEADS):
            p = h // 2
            kt = k_ref[0, pl.ds(r0, tk), p * LANES:(p + 1) * LANES]
            s = lax.dot_general(kt, qm_ref[h], _NT, preferred_element_type=F32) + bias
            s_ref[h] = s
            al_ref[h:h + 1, :] = jnp.max(s, axis=0, keepdims=True)
        for h in range(DSA_HEADS):
            m_old = m_ref[h:h + 1, :]
            m_new = jnp.maximum(m_old, al_ref[h:h + 1, :])
            p_ref[h] = jnp.exp2(s_ref[h] - m_new).astype(BF16)
            m_ref[h:h + 1, :] = m_new
            al_ref[h:h + 1, :] = jnp.exp2(m_old - m_new)
        ones = jnp.ones((PACKED_SUBLANES, tk), BF16)
        for h in range(DSA_HEADS):
            vt = jnp.concatenate([vT_ref[0, t, h * DSA_DH:(h + 1) * DSA_DH, :], ones], axis=0)
            acc_ref[h] = al_ref[h:h + 1, :] * acc_ref[h] + jnp.dot(
                vt, p_ref[h], preferred_element_type=F32)
        return carry

    lax.fori_loop(0, nt, att_tile, 0)
    for h in range(DSA_HEADS):
        oT_ref[0, h * DSA_DH:(h + 1) * DSA_DH, :] = (
            acc_ref[h, :DSA_DH, :] / acc_ref[h, DSA_DH:DSA_DH + 1, :]).astype(oT_ref.dtype)


def _dsa(dq, iq, iwT, dk, dvT, ik2):
    bsz, t, _ = dq.shape
    tq, tk = DSA_TQ, DSA_TK
    assert tq == tk and t % tq == 0
    nt = t // tk
    assert nt % 2 == 0
    topk = min(TOPK_MAX, t // 4)
    row = lambda w: pl.BlockSpec((1, tq, w), lambda b, i: (b, i, 0))
    return pl.pallas_call(
        functools.partial(_dsa_kernel, topk=topk),
        name="dsa",
        grid=(bsz, t // tq),
        in_specs=[
            row(DSA_WIDTH), row(IDX_HEADS * IDX_DIM),
            pl.BlockSpec((1, IDX_HEADS, tq), lambda b, i: (b, 0, i)),
            pl.BlockSpec((1, t, DSA_WIDTH), lambda b, i: (b, 0, 0)),
            pl.BlockSpec((1, nt, DSA_WIDTH, tk), lambda b, i: (b, 0, 0, 0)),
            pl.BlockSpec((1, t, 2 * IDX_DIM), lambda b, i: (b, 0, 0)),
        ],
        out_specs=pl.BlockSpec((1, DSA_WIDTH, tq), lambda b, i: (b, 0, i)),
        out_shape=jax.ShapeDtypeStruct((bsz, DSA_WIDTH, t), BF16),
        scratch_shapes=[
            pltpu.VMEM((nt, tk, tq), F32),
            pltpu.VMEM((nt, tk, tq), BF16),
            pltpu.VMEM((nt, tk, tq), F32),
            pltpu.VMEM((DSA_HEADS, tq, LANES), BF16),
            pltpu.VMEM((IDX_HEADS, tq, LANES), BF16),
            pltpu.VMEM((DSA_HEADS, tk, tq), F32),
            pltpu.VMEM((DSA_HEADS, tk, tq), BF16),
            pltpu.VMEM((DSA_HEADS, tq), F32),
            pltpu.VMEM((DSA_HEADS, tq), F32),
            pltpu.VMEM((DSA_HEADS, DSA_DH + PACKED_SUBLANES, tq), F32),
        ],
        compiler_params=pltpu.CompilerParams(
            dimension_semantics=("arbitrary", "arbitrary"), vmem_limit_bytes=VMEM_LIMIT_BYTES),
    )(dq, iq, iwT, dk, dvT, ik2)


def kernel(x, w_in, w_gla_a2, b_gla_a, gla_norm_g, w_out, ffn1_w_gu, ffn1_w_down,
           ffn2_w_gu, ffn2_w_down, ln1_g, ln1_b, ln2_g, ln2_b, ln3_g, ln3_b):
    bsz, t, d = x.shape
    n = bsz * t
    for l in range(DEPTH):
        x1 = _ffn_ln(x.reshape(n, d), ffn1_w_gu[l], ffn1_w_down[l], ln1_g[l], ln1_b[l])
        gq, gk, la, gv, gg, dq, dk, iq, ik2, dvT, iwT = _in_proj(
            x1.reshape(bsz, t, d), w_in[l], w_gla_a2[l], b_gla_a[l])
        o_gla = _gla(gq, gk, la, gv, gg, gla_norm_g[l])
        o_dsaT = _dsa(dq, iq, iwT, dk, dvT, ik2)
        x = _ffn_ln(x1, ffn2_w_gu[l], ffn2_w_down[l], ln3_g[l], ln3_b[l],
                    mix=(o_gla.reshape(n, GLA_WIDTH), o_dsaT, w_out[l], ln2_g[l], ln2_b[l]),
                    ).reshape(bsz, t, d)
    return x
```
